```python
import math
import jax, jax.numpy as jnp
from jax import lax
import numpy as np

D_MODEL = 2048
BATCH = 32
SEQ = 256
DEPTH = 4
DEC_BATCH = 4
DEC_SEQ = 2048
PAST_LEN = 256

GRID_W = 64
EPS = 1e-6
N_BRANCH = 3
HY_WIDTH = 1024
HY_ORDER = 2
HY_SHORT = 3
HY_FREQS = 16
HY_FEAT = 1 + 2 * HY_FREQS
HY_FHID = 64
MLA_HEADS = 16
QK_NOPE = 64
QK_ROPE = 32
QK_HEAD = QK_NOPE + QK_ROPE
V_HEAD = 64
Q_LORA = 512
KV_LORA = 256
MLA_WIDTH = MLA_HEADS * V_HEAD
Q_BLOCK = 128
ROPE_BASE = 10000.0
CF_WIDTH = 1024
CF_KERNEL = 31
N_GROUPS = 4
EXP_PER_GROUP = 4
N_EXPERTS = N_GROUPS * EXP_PER_GROUP
TOP_K_IN_GROUP = 2
D_EXPERT = 512
IN_SPLITS = (3 * HY_WIDTH, Q_LORA, KV_LORA, QK_ROPE, 2 * CF_WIDTH, N_BRANCH * D_MODEL)
IN_COLS = sum(IN_SPLITS)

kernel_name = 'hyena_mla_conformer_hmoe_dit_step'

F32 = jnp.float32


def rms_norm(x, g):
    xf = x.astype(F32)
    y = xf * lax.rsqrt(jnp.mean(xf * xf, axis=-1, keepdims=True) + EPS)
    return (y * g.astype(F32)).astype(x.dtype)


def layer_norm(x, g, b):
    xf = x.astype(F32)
    mu = jnp.mean(xf, axis=-1, keepdims=True)
    var = jnp.mean(jnp.square(xf - mu), axis=-1, keepdims=True)
    return ((xf - mu) * lax.rsqrt(var + EPS) * g.astype(F32) + b.astype(F32)).astype(x.dtype)


def split_columns(p):
    out = []
    off = 0
    for s in IN_SPLITS:
        out.append(p[..., off:off + s])
        off += s
    return out


def dw_conv(x, w):
    k = w.shape[0]
    pad = (k - 1) // 2
    return lax.conv_general_dilated(x, w[:, None, :].astype(x.dtype), window_strides=(1,),
                                    padding=[(pad, pad)], dimension_numbers=('NWC', 'WIO', 'NWC'),
                                    feature_group_count=x.shape[-1])


def axial_rope(L):
    rows = L // GRID_W
    row = jnp.repeat(jnp.arange(rows, dtype=F32), GRID_W)
    col = jnp.tile(jnp.arange(GRID_W, dtype=F32), rows)
    n_freq = QK_ROPE // 4
    inv = jnp.power(ROPE_BASE, -jnp.arange(n_freq, dtype=F32) / n_freq)
    ang = jnp.concatenate([row[:, None] * inv[None, :], col[:, None] * inv[None, :]], axis=-1)
    return jnp.cos(ang), jnp.sin(ang)


def apply_rope(x, rope):
    cos, sin = rope
    cos = cos[None, :, None, :]
    sin = sin[None, :, None, :]
    xf = x.astype(F32)
    x1, x2 = xf[..., :QK_ROPE // 2], xf[..., QK_ROPE // 2:]
    return jnp.concatenate([x1 * cos - x2 * sin, x1 * sin + x2 * cos], axis=-1).astype(x.dtype)


def adaln(cond, w_mod, b_mod):
    mod = jax.nn.silu(cond) @ w_mod + b_mod
    return jnp.split(mod[:, None, :], 6, axis=-1)


def hyena_filters(L, lp):
    tn = jnp.arange(L, dtype=F32) / L
    freqs = jnp.arange(1, HY_FREQS + 1, dtype=F32)
    ang = (2.0 * math.pi) * tn[:, None] * freqs[None, :]
    feat = jnp.concatenate([tn[:, None], jnp.cos(ang), jnp.sin(ang)], axis=-1)
    h = jnp.sin(feat @ lp['hy_f_w1'].astype(F32) + lp['hy_f_b1'].astype(F32))
    h = jnp.sin(h @ lp['hy_f_w2'].astype(F32) + lp['hy_f_b2'].astype(F32))
    f = (h @ lp['hy_f_w3'].astype(F32)) * jnp.exp(-tn[:, None] * lp['hy_decay'].astype(F32)[None, :])
    return f.reshape(L, HY_ORDER, 2, HY_WIDTH)


def bidir_long_conv(u, hf, hb):
    L = u.shape[1]
    k = jnp.concatenate([hf, jnp.zeros_like(hf[:1]), hb[1:][::-1]], axis=0)
    U = jnp.fft.rfft(u.astype(F32), n=2 * L, axis=1)
    K = jnp.fft.rfft(k, axis=0)
    y = jnp.fft.irfft(U * K[None], n=2 * L, axis=1)[:, :L]
    return y.astype(u.dtype)


def hyena_branch(u, lp):
    L = u.shape[1]
    u = dw_conv(u, lp['hy_short'])
    v, x1, x2 = jnp.split(u, 3, axis=-1)
    filt = hyena_filters(L, lp)
    gates = (x1, x2)
    z = v
    for n in range(HY_ORDER):
        z = gates[n] * (bidir_long_conv(z, filt[:, n, 0], filt[:, n, 1]) + z * lp['hy_bias'][n])
    return z @ lp['w_hy_out']


def mla_queries(cq, lp, rope):
    B, L, _ = cq.shape
    q = (rms_norm(cq, lp['q_a_norm']) @ lp['w_uq']).reshape(B, L, MLA_HEADS, QK_HEAD)
    q = rms_norm(q, lp['q_norm'])
    if rope is not None:
        q = jnp.concatenate([q[..., :QK_NOPE], apply_rope(q[..., QK_NOPE:], rope)], axis=-1)
    return q


def mla_keys_values(ckv_n, kpe, lp, rope):
    B, L, _ = ckv_n.shape
    kv = (ckv_n @ lp['w_ukv']).reshape(B, L, MLA_HEADS, QK_NOPE + V_HEAD)
    k_nope, v = kv[..., :QK_NOPE], kv[..., QK_NOPE:]
    k_pe = jnp.broadcast_to(kpe[:, :, None, :], (B, L, MLA_HEADS, QK_ROPE)).astype(k_nope.dtype)
    k = rms_norm(jnp.concatenate([k_nope, k_pe], axis=-1), lp['k_norm'])
    if rope is not None:
        k = jnp.concatenate([k[..., :QK_NOPE], apply_rope(k[..., QK_NOPE:], rope)], axis=-1)
    return k, v


def blocked_attention(q, k, v):
    B, Lq, H, Dk = q.shape
    nb = Lq // Q_BLOCK
    qb = q.reshape(B, nb, Q_BLOCK, H, Dk).transpose(1, 0, 2, 3, 4)
    scale = Dk ** -0.5

    def one_block(qblk):
        s = jnp.einsum('bqhd,bkhd->bhqk', qblk, k).astype(F32) * scale
        p = jax.nn.softmax(s, axis=-1).astype(v.dtype)
        return jnp.einsum('bhqk,bkhd->bqhd', p, v)

    o = lax.map(one_block, qb)
    return o.transpose(1, 0, 2, 3, 4).reshape(B, Lq, H * v.shape[-1])


def conformer_branch(u, lp):
    a, g = jnp.split(u, 2, axis=-1)
    h = a * jax.nn.sigmoid(g)
    h = dw_conv(h, lp['cf_dw']) + lp['cf_dw_b']
    h = jax.nn.silu(layer_norm(h, lp['cf_ln_g'], lp['cf_ln_b']))
    return h @ lp['w_cf_out'] + lp['b_cf_out']


def hier_moe(h, lp):
    B, L, D = h.shape
    t = h.reshape(B * L, D)
    n = t.shape[0]
    pg = jax.nn.softmax((t @ lp['w_rg']).astype(F32) + lp['b_rg'].astype(F32), axis=-1)
    pg_top, g_top = lax.top_k(pg, 1)
    le = ((t @ lp['w_re']).astype(F32) + lp['b_re'].astype(F32)).reshape(n, N_GROUPS, EXP_PER_GROUP)
    le_sel = le[jnp.arange(n), g_top[:, 0]]
    pe = jax.nn.softmax(le_sel, axis=-1)
    pe_top, e_top = lax.top_k(pe, TOP_K_IN_GROUP)
    w = pg_top * (pe_top / jnp.sum(pe_top, axis=-1, keepdims=True))
    expert_id = g_top * EXP_PER_GROUP + e_top
    combine = jnp.sum(jax.nn.one_hot(expert_id, N_EXPERTS, dtype=F32) * w[..., None], axis=1)
    hg = jnp.einsum('nd,edf->nef', t, lp['w_eg'])
    hu = jnp.einsum('nd,edf->nef', t, lp['w_eu'])
    a = jax.nn.silu(hg) * hu * combine[..., None].astype(hg.dtype)
    y = jnp.einsum('nef,efd->nd', a, lp['w_ed'])
    return y.reshape(B, L, D)


def trunk_layer(x, cond, lp, ctx_ckv=None, ctx_kpe=None):
    B, L, _ = x.shape
    is_latent = ctx_ckv is not None
    sh1, sc1, g1, sh2, sc2, g2 = adaln(cond, lp['w_mod'], lp['b_mod'])
    h = rms_norm(x, lp['norm_mix']) * (1 + sc1) + sh1
    hy_in, cq, ckv, kpe, cf_in, gate_in = split_columns(h @ lp['w_in'])
    y_a = hyena_branch(hy_in, lp)
    rope = axial_rope(L) if is_latent else None
    ckv_n = rms_norm(ckv, lp['kv_a_norm'])
    q = mla_queries(cq, lp, rope)
    k, v = mla_keys_values(ckv_n, kpe, lp, rope)
    if is_latent:
        k_c, v_c = mla_keys_values(ctx_ckv, ctx_kpe, lp, None)
        k = jnp.concatenate([k, k_c.astype(k.dtype)], axis=1)
        v = jnp.concatenate([v, v_c.astype(v.dtype)], axis=1)
    y_b = blocked_attention(q, k, v) @ lp['w_mla_out']
    y_c = conformer_branch(cf_in, lp)
    ga, gb, gc = jnp.split(jax.nn.sigmoid(gate_in.astype(F32)).astype(x.dtype), N_BRANCH, axis=-1)
    merged = ga * y_a + gb * y_b + gc * y_c
    x = x + g1 * (merged @ lp['w_out'])
    h2 = rms_norm(x, lp['norm_ffn']) * (1 + sc2) + sh2
    x = x + g2 * hier_moe(h2, lp)
    return x, ckv_n, kpe


def setup_inputs(seed: int = 0) -> dict:
    key = jax.random.key(seed)
    ks = iter(jax.random.split(key, 64))

    def nrm(shape, scale):
        return jax.random.normal(next(ks), shape, F32) * scale

    def gain(shape):
        return 1.0 + nrm(shape, 0.02)

    D = D_MODEL
    return {
        'x_prompt': nrm((BATCH, SEQ, D), 1.0),
        'x_sample': nrm((DEC_BATCH, DEC_SEQ, D), 1.0),
        'cache_ckv': nrm((DEC_BATCH, DEPTH, PAST_LEN, KV_LORA), 1.0),
        'cache_kpe': nrm((DEC_BATCH, DEPTH, PAST_LEN, QK_ROPE), 1.0),
        'c': nrm((DEC_BATCH, D), 1.0),
        'c_ctx': nrm((D,), 1.0),
        'w_mod': nrm((DEPTH, D, 6 * D), 0.5 * D ** -0.5),
        'b_mod': nrm((DEPTH, 6 * D), 0.02),
        'norm_mix': gain((DEPTH, D)),
        'norm_ffn': gain((DEPTH, D)),
        'w_in': nrm((DEPTH, D, IN_COLS), D ** -0.5),
        'hy_short': nrm((DEPTH, HY_SHORT, 3 * HY_WIDTH), HY_SHORT ** -0.5),
        'hy_f_w1': nrm((DEPTH, HY_FEAT, HY_FHID), HY_FEAT ** -0.5),
        'hy_f_b1': nrm((DEPTH, HY_FHID), 0.02),
        'hy_f_w2': nrm((DEPTH, HY_FHID, HY_FHID), HY_FHID ** -0.5),
        'hy_f_b2': nrm((DEPTH, HY_FHID), 0.02),
        'hy_f_w3': nrm((DEPTH, HY_FHID, 2 * HY_ORDER * HY_WIDTH), 0.05 * HY_FHID ** -0.5),
        'hy_decay': jnp.exp(jax.random.uniform(next(ks), (DEPTH, 2 * HY_ORDER * HY_WIDTH), F32,
                                               minval=math.log(2.0), maxval=math.log(64.0))),
        'hy_bias': nrm((DEPTH, HY_ORDER, HY_WIDTH), 1.0),
        'w_hy_out': nrm((DEPTH, HY_WIDTH, D), HY_WIDTH ** -0.5),
        'q_a_norm': gain((DEPTH, Q_LORA)),
        'w_uq': nrm((DEPTH, Q_LORA, MLA_HEADS * QK_HEAD), Q_LORA ** -0.5),
        'kv_a_norm': gain((DEPTH, KV_LORA)),
        'w_ukv': nrm((DEPTH, KV_LORA, MLA_HEADS * (QK_NOPE + V_HEAD)), KV_LORA ** -0.5),
        'q_norm': gain((DEPTH, QK_HEAD)),
        'k_norm': gain((DEPTH, QK_HEAD)),
        'w_mla_out': nrm((DEPTH, MLA_WIDTH, D), MLA_WIDTH ** -0.5),
        'cf_dw': nrm((DEPTH, CF_KERNEL, CF_WIDTH), CF_KERNEL ** -0.5),
        'cf_dw_b': nrm((DEPTH, CF_WIDTH), 0.02),
        'cf_ln_g': gain((DEPTH, CF_WIDTH)),
        'cf_ln_b': nrm((DEPTH, CF_WIDTH), 0.02),
        'w_cf_out': nrm((DEPTH, CF_WIDTH, D), CF_WIDTH ** -0.5),
        'b_cf_out': nrm((DEPTH, D), 0.02),
        'w_out': nrm((DEPTH, D, D), D ** -0.5),
        'w_rg': nrm((DEPTH, D, N_GROUPS), D ** -0.5),
        'b_rg': nrm((DEPTH, N_GROUPS), 0.01),
        'w_re': nrm((DEPTH, D, N_EXPERTS), D ** -0.5),
        'b_re': nrm((DEPTH, N_EXPERTS), 0.01),
        'w_eg': nrm((DEPTH, N_EXPERTS, D, D_EXPERT), D ** -0.5),
        'w_eu': nrm((DEPTH, N_EXPERTS, D, D_EXPERT), D ** -0.5),
        'w_ed': nrm((DEPTH, N_EXPERTS, D_EXPERT, D), D_EXPERT ** -0.5),
    }


def reference(x_prompt, x_sample, cache_ckv, cache_kpe, c, c_ctx, w_mod, b_mod, norm_mix, norm_ffn,
              w_in, hy_short, hy_f_w1, hy_f_b1, hy_f_w2, hy_f_b2, hy_f_w3, hy_decay, hy_bias, w_hy_out,
              q_a_norm, w_uq, kv_a_norm, w_ukv, q_norm, k_norm, w_mla_out,
              cf_dw, cf_dw_b, cf_ln_g, cf_ln_b, w_cf_out, b_cf_out, w_out,
              w_rg, b_rg, w_re, b_re, w_eg, w_eu, w_ed):
    layers = [dict(w_mod=w_mod[l], b_mod=b_mod[l], norm_mix=norm_mix[l], norm_ffn=norm_ffn[l],
                   w_in=w_in[l], hy_short=hy_short[l], hy_f_w1=hy_f_w1[l], hy_f_b1=hy_f_b1[l],
                   hy_f_w2=hy_f_w2[l], hy_f_b2=hy_f_b2[l], hy_f_w3=hy_f_w3[l], hy_decay=hy_decay[l],
                   hy_bias=hy_bias[l], w_hy_out=w_hy_out[l], q_a_norm=q_a_norm[l], w_uq=w_uq[l],
                   kv_a_norm=kv_a_norm[l], w_ukv=w_ukv[l], q_norm=q_norm[l], k_norm=k_norm[l],
                   w_mla_out=w_mla_out[l], cf_dw=cf_dw[l], cf_dw_b=cf_dw_b[l], cf_ln_g=cf_ln_g[l],
                   cf_ln_b=cf_ln_b[l], w_cf_out=w_cf_out[l], b_cf_out=b_cf_out[l], w_out=w_out[l],
                   w_rg=w_rg[l], b_rg=b_rg[l], w_re=w_re[l], b_re=b_re[l],
                   w_eg=w_eg[l], w_eu=w_eu[l], w_ed=w_ed[l])
              for l in range(DEPTH)]

    x = x_prompt
    ckv_list = []
    kpe_list = []
    for l in range(DEPTH):
        x, ckv_n, kpe = trunk_layer(x, c_ctx[None, :], layers[l])
        ckv_list.append(ckv_n)
        kpe_list.append(kpe)
    y_prompt = x
    new_ckv = jnp.stack(ckv_list, axis=1)
    new_kpe = jnp.stack(kpe_list, axis=1)

    x = x_sample
    for l in range(DEPTH):
        x, _, _ = trunk_layer(x, c, layers[l], cache_ckv[:, l], cache_kpe[:, l])
    y_sample = x

    return (y_prompt, y_sample, new_ckv, new_kpe)
```

```python
import functools
import math
from typing import NamedTuple

import jax
import jax.numpy as jnp
from jax import lax
from jax.experimental import pallas as pl
from jax.experimental.pallas import tpu as pltpu

F32 = jnp.float32
BF16 = jnp.bfloat16
EPS = 1e-6
LANE = 128
SUBLANE = 8
CONV_SUB = 256
VMEM_LIMIT_BYTES = 56 << 20
ROPE_BASE = 10000.0
HY_ORDER = 2
HY_SHORT = 3
HALO = 16
CONV_CHUNK = 16


class Dims(NamedTuple):
    d: int = 2048
    batch: int = 32
    seq: int = 256
    depth: int = 4
    dec_batch: int = 4
    dec_seq: int = 2048
    past: int = 256
    grid_w: int = 64
    hw: int = 1024
    hy_freqs: int = 16
    hy_fhid: int = 64
    heads: int = 16
    nope: int = 64
    rope: int = 32
    vh: int = 64
    q_lora: int = 512
    kv_lora: int = 256
    cw: int = 1024
    cf_k: int = 31
    groups: int = 4
    epg: int = 4
    fe: int = 512
    tm: int = 1024
    tm_s: int = 512
    tn: int = 1024
    tq: int = 512
    tc_hy: int = 256

    @property
    def n_ctx(self):
        return self.batch * self.seq

    @property
    def n_lat(self):
        return self.dec_batch * self.dec_seq

    @property
    def nt(self):
        return self.n_ctx + self.n_lat

    @property
    def n_exp(self):
        return self.groups * self.epg

    @property
    def dk(self):
        return self.nope + self.rope


def _cparams(*sem):
    return pltpu.CompilerParams(dimension_semantics=sem, vmem_limit_bytes=VMEM_LIMIT_BYTES)


def _resident(shape, index_map):
    return pl.BlockSpec(shape, index_map, pipeline_mode=pl.Buffered(1))


def _silu(x):
    return x * jax.nn.sigmoid(x)


def _rms(x, g):
    return x * lax.rsqrt(jnp.mean(x * x, axis=-1, keepdims=True) + EPS) * g


def _cond_id(dm, tile):
    n_ctx_tiles = dm.n_ctx // tile
    per_seq = dm.dec_seq // tile
    return lambda i: jnp.where(i < n_ctx_tiles, 0, 1 + (i - n_ctx_tiles) // per_seq)


def _rope_blk(dm, tile):
    n_ctx_tiles = dm.n_ctx // tile
    per_seq = dm.dec_seq // tile
    return lambda i: jnp.where(i < n_ctx_tiles, 0, 1 + (i - n_ctx_tiles) % per_seq)


def _adaln_kernel(c_ref, w_ref, b_ref, o_ref):
    a = _silu(c_ref[...]).astype(BF16)
    o_ref[0] = jnp.dot(a, w_ref[0].astype(BF16), preferred_element_type=F32) + b_ref[0]


def adaln(dm, cond8, w_mod, b_mod):
    depth, d, n = w_mod.shape
    tn = min(n, 1024)
    return pl.pallas_call(
        _adaln_kernel,
        out_shape=jax.ShapeDtypeStruct((depth, 8, n), F32),
        grid=(depth, n // tn),
        in_specs=[pl.BlockSpec((8, d), lambda l, j: (0, 0)),
                  pl.BlockSpec((1, d, tn), lambda l, j: (l, 0, j)),
                  pl.BlockSpec((1, 1, tn), lambda l, j: (l, 0, j))],
        out_specs=pl.BlockSpec((1, 8, tn), lambda l, j: (l, 0, j)),
        compiler_params=_cparams("arbitrary", "arbitrary"),
        name="adaln",
    )(cond8, w_mod, b_mod.reshape(depth, 1, n))


def _inproj_kernel(x_ref, mod_ref, g_ref, w_ref, p2_ref, p_ref, hn_ref, *, d):
    j = pl.program_id(1)

    @pl.when(j == 0)
    def _():
        y = _rms(x_ref[...], g_ref[...])
        sh = mod_ref[0, :, 0:d]
        sc = mod_ref[0, :, d:2 * d]
        hn_ref[...] = (y * (1.0 + sc) + sh).astype(BF16)

    acc = jnp.dot(hn_ref[...], w_ref[...], preferred_element_type=F32)
    p_ref[...] = acc.astype(BF16)

    @pl.when(j == 0)
    def _():
        p2_ref[...] = acc


def inproj(dm, x, mod_l, g, w_packed):
    nt, d = x.shape
    tm, tn = dm.tm, dm.tn
    ncols = w_packed.shape[1]
    nj = ncols // tn
    cid = _cond_id(dm, tm)
    return pl.pallas_call(
        functools.partial(_inproj_kernel, d=d),
        out_shape=(jax.ShapeDtypeStruct((nt, tn), F32),
                   jax.ShapeDtypeStruct((nt, ncols - tn), BF16)),
        grid=(nt // tm, nj),
        in_specs=[pl.BlockSpec((tm, d), lambda i, j: (i, 0)),
                  pl.BlockSpec((1, 1, 6 * d), lambda i, j: (cid(i), 0, 0)),
                  pl.BlockSpec((1, d), lambda i, j: (0, 0)),
                  pl.BlockSpec((d, tn), lambda i, j: (0, j))],
        out_specs=(pl.BlockSpec((tm, tn), lambda i, j: (i, 0)),
                   pl.BlockSpec((tm, tn), lambda i, j: (i, jnp.maximum(j - 1, 0)))),
        scratch_shapes=[pltpu.VMEM((tm, d), BF16)],
        compiler_params=_cparams("arbitrary", "arbitrary"),
        name="inproj",
    )(x, mod_l, g, w_packed)


def _dot3(a, b):
    ah = a.astype(BF16)
    al = (a - ah.astype(F32)).astype(BF16)
    bh = b.astype(BF16)
    bl = (b - bh.astype(F32)).astype(BF16)
    return (jnp.dot(ah, bh, preferred_element_type=F32) + jnp.dot(al, bh, preferred_element_type=F32)
            + jnp.dot(ah, bl, preferred_element_type=F32))


def _hyfilt_kernel(w1_ref, b1_ref, w2_ref, b2_ref, w3_ref, dec_ref, f_ref, *, length, nfreq):
    row = lax.broadcasted_iota(jnp.int32, (length, LANE), 0).astype(F32)
    lane = lax.broadcasted_iota(jnp.int32, (length, LANE), 1)
    tn = row / float(length)
    fr = jnp.where(lane <= nfreq, lane, lane - nfreq).astype(F32)
    ang = (2.0 * math.pi) * tn * fr
    feat = jnp.where(lane == 0, tn, jnp.where(lane <= nfreq, jnp.cos(ang),
                                              jnp.where(lane <= 2 * nfreq, jnp.sin(ang), 0.0)))
    h = jnp.sin(_dot3(feat, w1_ref[...]) + b1_ref[...])
    h = jnp.sin(_dot3(h, w2_ref[...]) + b2_ref[...])
    f_ref[...] = _dot3(h, w3_ref[...]) * jnp.exp(-tn[:, 0:1] * dec_ref[...])


def hyena_filters(dm, length, w1p, b1, w2, b2, w3, decay):
    ncol = w3.shape[1]
    tn = min(ncol, 1024)
    fh = w2.shape[0]
    return pl.pallas_call(
        functools.partial(_hyfilt_kernel, length=length, nfreq=dm.hy_freqs),
        out_shape=jax.ShapeDtypeStruct((length, ncol), F32),
        grid=(ncol // tn,),
        in_specs=[pl.BlockSpec((LANE, fh), lambda j: (0, 0)),
                  pl.BlockSpec((1, fh), lambda j: (0, 0)),
                  pl.BlockSpec((fh, fh), lambda j: (0, 0)),
                  pl.BlockSpec((1, fh), lambda j: (0, 0)),
                  pl.BlockSpec((fh, tn), lambda j: (0, j)),
                  pl.BlockSpec((1, tn), lambda j: (0, j))],
        out_specs=pl.BlockSpec((length, tn), lambda j: (0, j)),
        compiler_params=_cparams("arbitrary"),
        name="hyena_filters",
    )(w1p, b1, w2, b2, w3, decay)


def _hyspec_kernel(hf_ref, hb_ref, c_ref, s_ref, kc_ref, ks_ref, kn_ref, *, length):
    row = lax.broadcasted_iota(jnp.int32, (length, 1), 0)
    hf = hf_ref[...]
    hb0 = jnp.where(row == 0, 0.0, hb_ref[...])
    sp = hf + hb0
    sm = hf - hb0
    inv = 1.0 / length
    kc = jnp.dot(c_ref[...], sp.astype(BF16), preferred_element_type=F32)
    kc_ref[0] = kc * jnp.where(row == 0, 0.5 * inv, inv)
    ks_ref[0] = jnp.dot(s_ref[...], sm.astype(BF16), preferred_element_type=F32) * inv
    sgn = (1 - 2 * (row & 1)).astype(F32)
    kn = jnp.sum(sgn * sp, axis=0, keepdims=True) * (0.5 * inv)
    kn_ref[0] = jnp.broadcast_to(kn, kn_ref.shape[1:])


def hyena_spectra(dm, length, filt, cmat, smat):
    hw = dm.hw
    tc = min(hw, 512)
    nct = hw // tc
    shp = jax.ShapeDtypeStruct((HY_ORDER, length, hw), F32)
    return pl.pallas_call(
        functools.partial(_hyspec_kernel, length=length),
        out_shape=(shp, shp, jax.ShapeDtypeStruct((HY_ORDER, 8, hw), F32)),
        grid=(HY_ORDER, nct),
        in_specs=[pl.BlockSpec((length, tc), lambda n, j: (0, (2 * n) * nct + j)),
                  pl.BlockSpec((length, tc), lambda n, j: (0, (2 * n + 1) * nct + j)),
                  _resident((length, length), lambda n, j: (0, 0)),
                  _resident((length, length), lambda n, j: (0, 0))],
        out_specs=(pl.BlockSpec((1, length, tc), lambda n, j: (n, 0, j)),
                   pl.BlockSpec((1, length, tc), lambda n, j: (n, 0, j)),
                   pl.BlockSpec((1, 8, tc), lambda n, j: (n, 0, j))),
        compiler_params=_cparams("arbitrary", "arbitrary"),
        name="hyena_spectra",
    )(filt, filt, cmat, smat)


def _hyconv_kernel(v_ref, x1_ref, x2_ref, swv_ref, sw1_ref, sw2_ref, bias_ref, kc_ref, ks_ref, kn_ref,
                   c_ref, s_ref, *rest, length):
    o_ref, zc_ref, zs_ref = rest[-3:]
    row = lax.broadcasted_iota(jnp.int32, (length, 1), 0)
    sgn = (1 - 2 * (row & 1)).astype(F32)

    def short(x_ref, w_ref):
        x = x_ref[...].astype(F32)
        xp = jnp.where(row == 0, 0.0, pltpu.roll(x, 1, 0))
        xn = jnp.where(row == length - 1, 0.0, pltpu.roll(x, length - 1, 0))
        return w_ref[0:1, :] * xp + w_ref[1:2, :] * x + w_ref[2:3, :] * xn

    def longconv(z, n):
        zb = z.astype(BF16)
        uc = jnp.dot(c_ref[...], zb, preferred_element_type=F32)
        us = jnp.dot(s_ref[...], zb, preferred_element_type=F32)
        un = jnp.sum(sgn * z, axis=0, keepdims=True)
        kc = kc_ref[n]
        ks = ks_ref[n]
        zc_ref[...] = (uc * kc - us * ks).astype(BF16)
        zs_ref[...] = (uc * ks + us * kc).astype(BF16)
        y = jnp.dot(c_ref[...], zc_ref[...], preferred_element_type=F32)
        y = y + jnp.dot(s_ref[...], zs_ref[...], preferred_element_type=F32)
        return y + sgn * (un * kn_ref[n, 0:1, :])

    v = short(v_ref, swv_ref)
    z = short(x1_ref, sw1_ref) * (longconv(v, 0) + v * bias_ref[0:1, :])
    z = short(x2_ref, sw2_ref) * (longconv(z, 1) + z * bias_ref[1:2, :])
    o_ref[...] = z.astype(BF16)


def hyena_conv(dm, p, hy_col0, row0, nseq, length, tc, short_w, bias, kc, ks, kn, cmat, smat, prev=None):
    hw = dm.hw
    nct = hw // tc
    rb0 = row0 // length
    cb0 = hy_col0 // tc

    def part(k):
        return pl.BlockSpec((length, tc), lambda j, s, k=k: (rb0 + s, cb0 + k * nct + j))

    def sw(k):
        return pl.BlockSpec((HY_SHORT, tc), lambda j, s, k=k: (0, k * nct + j))

    in_specs = [part(0), part(1), part(2), sw(0), sw(1), sw(2),
                pl.BlockSpec((HY_ORDER, tc), lambda j, s: (0, j)),
                _resident((HY_ORDER, length, tc), lambda j, s: (0, 0, j)),
                _resident((HY_ORDER, length, tc), lambda j, s: (0, 0, j)),
                pl.BlockSpec((HY_ORDER, 8, tc), lambda j, s: (0, 0, j)),
                _resident((length, length), lambda j, s: (0, 0)),
                _resident((length, length), lambda j, s: (0, 0))]
    args = [p, p, p, short_w, short_w, short_w, bias, kc, ks, kn, cmat, smat]
    aliases = {}
    if prev is not None:
        in_specs.append(pl.BlockSpec(memory_space=pl.ANY))
        args.append(prev)
        aliases = {len(args) - 1: 0}
    return pl.pallas_call(
        functools.partial(_hyconv_kernel, length=length),
        out_shape=jax.ShapeDtypeStruct((dm.nt, hw), BF16),
        grid=(nct, nseq),
        in_specs=in_specs,
        out_specs=pl.BlockSpec((length, tc), lambda j, s: (rb0 + s, j)),
        scratch_shapes=[pltpu.VMEM((length, tc), BF16), pltpu.VMEM((length, tc), BF16)],
        input_output_aliases=aliases,
        compiler_params=_cparams("arbitrary", "arbitrary"),
        name="hyena_conv_%d" % length,
    )(*args)


def _head_norm_rope(xh, gain, cos, sa, sb, inv_dk, half):
    ss = jnp.sum(xh * xh, axis=-1, keepdims=True) * inv_dk
    xh = xh * lax.rsqrt(ss + EPS) * gain
    return xh * cos + pltpu.roll(xh, half, 1) * sa + pltpu.roll(xh, LANE - half, 1) * sb


def _q_kernel(cq_ref, ga_ref, w_ref, gh_ref, cos_ref, sa_ref, sb_ref, o_ref, *, heads, dk, half):
    y = _rms(cq_ref[...], ga_ref[...])
    q = jnp.dot(y.astype(BF16), w_ref[...], preferred_element_type=F32)
    cos, sa, sb, gh = cos_ref[...], sa_ref[...], sb_ref[...], gh_ref[...]
    for h in range(heads):
        qh = _head_norm_rope(q[:, h * LANE:(h + 1) * LANE], gh, cos, sa, sb, 1.0 / dk, half)
        o_ref[:, h * LANE:(h + 1) * LANE] = qh.astype(BF16)


def mla_q(dm, p2, ga, w_uq, gh, cos_t, sa_t, sb_t):
    tm = dm.tm_s
    rb = _rope_blk(dm, tm)
    tab = pl.BlockSpec((tm, LANE), lambda i: (rb(i), 0))
    return pl.pallas_call(
        functools.partial(_q_kernel, heads=dm.heads, dk=dm.dk, half=dm.rope // 2),
        out_shape=jax.ShapeDtypeStruct((dm.nt, dm.heads * LANE), BF16),
        grid=(dm.nt // tm,),
        in_specs=[pl.BlockSpec((tm, dm.q_lora), lambda i: (i, 0)),
                  pl.BlockSpec((1, dm.q_lora), lambda i: (0, 0)),
                  pl.BlockSpec((dm.q_lora, dm.heads * LANE), lambda i: (0, 0)),
                  pl.BlockSpec((1, LANE), lambda i: (0, 0)),
                  tab, tab, tab],
        out_specs=pl.BlockSpec((tm, dm.heads * LANE), lambda i: (i, 0)),
        compiler_params=_cparams("arbitrary"),
        name="mla_q",
    )(p2, ga, w_uq, gh, cos_t, sa_t, sb_t)


def _kv_kernel(ckv_ref, kpe_ref, ga_ref, w_ref, gk_ref, cos_ref, sa_ref, sb_ref, k_ref, v_ref, cn_ref,
               *, heads, dk, half, normalize):
    x = ckv_ref[...]
    if normalize:
        x = _rms(x, ga_ref[...])
    cn_ref[...] = x
    kv = jnp.dot(x.astype(BF16), w_ref[...], preferred_element_type=F32)
    kpe = kpe_ref[...]
    cos, sa, sb, gk = cos_ref[...], sa_ref[...], sb_ref[...], gk_ref[...]
    for h in range(heads):
        kh = _head_norm_rope(kv[:, h * LANE:(h + 1) * LANE] + kpe, gk, cos, sa, sb, 1.0 / dk, half)
        k_ref[:, h * LANE:(h + 1) * LANE] = kh.astype(BF16)
    v_ref[...] = kv[:, heads * LANE:].astype(BF16)


def mla_kv(dm, ckv_src, ckv_blk, kpe_src, kpe_blk, nrows, tm, ga, w_ukv, gk, cos_t, sa_t, sb_t, rope_blk,
           normalize):
    hk = dm.heads * LANE
    hv = dm.heads * dm.vh
    tab = pl.BlockSpec((tm, LANE), lambda i: (rope_blk(i), 0))
    return pl.pallas_call(
        functools.partial(_kv_kernel, heads=dm.heads, dk=dm.dk, half=dm.rope // 2, normalize=normalize),
        out_shape=(jax.ShapeDtypeStruct((nrows, hk), BF16),
                   jax.ShapeDtypeStruct((nrows, hv), BF16),
                   jax.ShapeDtypeStruct((nrows, dm.kv_lora), F32)),
        grid=(nrows // tm,),
        in_specs=[pl.BlockSpec((tm, dm.kv_lora), lambda i: (i, ckv_blk)),
                  pl.BlockSpec((tm, LANE), lambda i: (i, kpe_blk)),
                  pl.BlockSpec((1, dm.kv_lora), lambda i: (0, 0)),
                  pl.BlockSpec((dm.kv_lora, hk + hv), lambda i: (0, 0)),
                  pl.BlockSpec((1, LANE), lambda i: (0, 0)),
                  tab, tab, tab],
        out_specs=(pl.BlockSpec((tm, hk), lambda i: (i, 0)),
                   pl.BlockSpec((tm, hv), lambda i: (i, 0)),
                   pl.BlockSpec((tm, dm.kv_lora), lambda i: (i, 0))),
        compiler_params=_cparams("arbitrary"),
        name="mla_kv_norm" if normalize else "mla_kv_cache",
    )(ckv_src, kpe_src, ga, w_ukv, gk, cos_t, sa_t, sb_t)


def _attn_kernel(q_ref, *refs, nseg, pairs, vh):
    o_ref = refs[-1]
    nt_dims = (((1,), (1,)), ((), ()))
    lane = lax.broadcasted_iota(jnp.int32, (q_ref.shape[0], LANE), 1)
    for p in range(pairs):
        outs = []
        for hh in range(2):
            c0 = (2 * p + hh) * LANE
            q = q_ref[:, c0:c0 + LANE]
            s = [lax.dot_general(q, refs[2 * i][:, c0:c0 + LANE], nt_dims, preferred_element_type=F32)
                 for i in range(nseg)]
            m = functools.reduce(jnp.maximum, [jnp.max(si, axis=-1, keepdims=True) for si in s])
            e = [jnp.exp(si - m) for si in s]
            l = functools.reduce(jnp.add, [jnp.sum(ei, axis=-1, keepdims=True) for ei in e])
            acc = functools.reduce(jnp.add, [
                jnp.dot(e[i].astype(BF16), refs[2 * i + 1][:, p * LANE:(p + 1) * LANE],
                        preferred_element_type=F32) for i in range(nseg)])
            outs.append(acc / l)
        o_ref[:, p * LANE:(p + 1) * LANE] = jnp.where(lane < vh, outs[0], outs[1]).astype(BF16)


def attention(dm, q, segs, row0, nseq, length, tq, pairs, prev=None):
    npg = dm.heads // 2 // pairs
    nqt = length // tq
    qb0 = row0 // tq
    in_specs = [pl.BlockSpec((tq, pairs * 2 * LANE), lambda s, g, t: (qb0 + s * nqt + t, g))]
    args = [q]
    for (k, v, lk, r0) in segs:
        kb0 = r0 // lk
        in_specs.append(pl.BlockSpec((lk, pairs * 2 * LANE), lambda s, g, t, kb0=kb0: (kb0 + s, g)))
        in_specs.append(pl.BlockSpec((lk, pairs * LANE), lambda s, g, t, kb0=kb0: (kb0 + s, g)))
        args += [k, v]
    aliases = {}
    if prev is not None:
        in_specs.append(pl.BlockSpec(memory_space=pl.ANY))
        args.append(prev)
        aliases = {len(args) - 1: 0}
    return pl.pallas_call(
        functools.partial(_attn_kernel, nseg=len(segs), pairs=pairs, vh=dm.vh),
        out_shape=jax.ShapeDtypeStruct((dm.nt, dm.heads * dm.vh), BF16),
        grid=(nseq, npg, nqt),
        in_specs=in_specs,
        out_specs=pl.BlockSpec((tq, pairs * LANE), lambda s, g, t: (qb0 + s * nqt + t, g)),
        input_output_aliases=aliases,
        compiler_params=_cparams("arbitrary", "arbitrary", "arbitrary"),
        name="attention_%d" % length,
    )(*args)


def _conf_kernel(a_ref, g_ref, w_ref, b_ref, lg_ref, lb_ref, *rest, length, ktaps, sub):
    o_ref, hs_ref, sh_ref = rest[-3:]
    cw = a_ref.shape[1]
    off = HALO - (ktaps - 1) // 2
    span = sub + SUBLANE * ((off + ktaps - 1) // SUBLANE)
    hs_ref[0:HALO, :] = jnp.zeros((HALO, cw), F32)
    hs_ref[HALO + length:, :] = jnp.zeros((HALO, cw), F32)
    hs_ref[HALO:HALO + length, :] = a_ref[...].astype(F32) * jax.nn.sigmoid(g_ref[...].astype(F32))
    bias, lg, lb = b_ref[...], lg_ref[...], lb_ref[...]

    for sb in range(length // sub):
        base = sb * sub
        for r in range(SUBLANE):
            sh_ref[r] = hs_ref[base + r:base + r + span, :]

        def body(c, carry):
            t0 = pl.multiple_of(c * CONV_CHUNK, CONV_CHUNK)
            acc = jnp.broadcast_to(bias, (CONV_CHUNK, cw))
            for j in range(ktaps):
                q, r = divmod(off + j, SUBLANE)
                acc = acc + w_ref[j:j + 1, :] * sh_ref[r, pl.ds(t0 + SUBLANE * q, CONV_CHUNK), :]
            mu = jnp.mean(acc, axis=-1, keepdims=True)
            xc = acc - mu
            var = jnp.mean(xc * xc, axis=-1, keepdims=True)
            y = xc * lax.rsqrt(var + EPS) * lg + lb
            o_ref[pl.ds(base + t0, CONV_CHUNK), :] = _silu(y).astype(BF16)
            return carry

        lax.fori_loop(0, sub // CONV_CHUNK, body, 0)


def conformer(dm, p, cf_col0, row0, nseq, length, w, b, lg, lb, prev=None):
    cw = dm.cw
    rb0 = row0 // length
    cb0 = cf_col0 // cw
    one = lambda s: (0, 0)
    in_specs = [pl.BlockSpec((length, cw), lambda s: (rb0 + s, cb0)),
                pl.BlockSpec((length, cw), lambda s: (rb0 + s, cb0 + 1)),
                pl.BlockSpec((dm.cf_k, cw), one), pl.BlockSpec((1, cw), one),
                pl.BlockSpec((1, cw), one), pl.BlockSpec((1, cw), one)]
    args = [p, p, w, b, lg, lb]
    aliases = {}
    if prev is not None:
        in_specs.append(pl.BlockSpec(memory_space=pl.ANY))
        args.append(prev)
        aliases = {len(args) - 1: 0}
    sub = min(length, CONV_SUB)
    span = sub + SUBLANE * ((HALO + (dm.cf_k - 1) // 2) // SUBLANE)
    return pl.pallas_call(
        functools.partial(_conf_kernel, length=length, ktaps=dm.cf_k, sub=sub),
        out_shape=jax.ShapeDtypeStruct((dm.nt, cw), BF16),
        grid=(nseq,),
        in_specs=in_specs,
        out_specs=pl.BlockSpec((length, cw), lambda s: (rb0 + s, 0)),
        scratch_shapes=[pltpu.VMEM((length + 2 * HALO, cw), F32),
                        pltpu.VMEM((SUBLANE, span, cw), F32)],
        input_output_aliases=aliases,
        compiler_params=_cparams("arbitrary"),
        name="conformer_%d" % length,
    )(*args)


def _merge_kernel(z_ref, o_ref, h_ref, ga_ref, gb_ref, gc_ref, wa_ref, wb_ref, wc_ref, bc_ref, m_ref):
    ya = jnp.dot(z_ref[...], wa_ref[...], preferred_element_type=F32)
    yb = jnp.dot(o_ref[...], wb_ref[...], preferred_element_type=F32)
    yc = jnp.dot(h_ref[...], wc_ref[...], preferred_element_type=F32) + bc_ref[...]
    sg = lambda r: jax.nn.sigmoid(r[...].astype(F32))
    m_ref[...] = (sg(ga_ref) * ya + sg(gb_ref) * yb + sg(gc_ref) * yc).astype(BF16)


def merge(dm, z2, o, hc, p, gate_col0, wa, wb, wc, bc):
    tm, d = dm.tm_s, dm.d
    gb0 = gate_col0 // d
    row = lambda i: (i, 0)
    one = lambda i: (0, 0)
    return pl.pallas_call(
        _merge_kernel,
        out_shape=jax.ShapeDtypeStruct((dm.nt, d), BF16),
        grid=(dm.nt // tm,),
        in_specs=[pl.BlockSpec((tm, dm.hw), row), pl.BlockSpec((tm, dm.heads * dm.vh), row),
                  pl.BlockSpec((tm, dm.cw), row),
                  pl.BlockSpec((tm, d), lambda i: (i, gb0)), pl.BlockSpec((tm, d), lambda i: (i, gb0 + 1)),
                  pl.BlockSpec((tm, d), lambda i: (i, gb0 + 2)),
                  _resident((dm.hw, d), one), _resident((dm.heads * dm.vh, d), one),
                  _resident((dm.cw, d), one), pl.BlockSpec((1, d), one)],
        out_specs=pl.BlockSpec((tm, d), row),
        compiler_params=_cparams("arbitrary"),
        name="merge",
    )(z2, o, hc, p, p, p, wa, wb, wc, bc)


def _outproj_kernel(m_ref, w_ref, x_ref, mod_ref, g_ref, wr_ref, br_ref, x1_ref, h2_ref, lg_ref, *, d):
    y = jnp.dot(m_ref[...], w_ref[...], preferred_element_type=F32)
    g1 = mod_ref[0, :, 2 * d:3 * d]
    sh2 = mod_ref[0, :, 3 * d:4 * d]
    sc2 = mod_ref[0, :, 4 * d:5 * d]
    x1 = x_ref[...] + g1 * y
    x1_ref[...] = x1
    h2 = (_rms(x1, g_ref[...]) * (1.0 + sc2) + sh2).astype(BF16)
    h2_ref[...] = h2
    lg_ref[...] = jnp.dot(h2, wr_ref[...], preferred_element_type=F32) + br_ref[...]


def outproj(dm, merged, w_out, x, mod_l, g, wr, br):
    tm, d = dm.tm_s, dm.d
    cid = _cond_id(dm, tm)
    row = lambda i: (i, 0)
    one = lambda i: (0, 0)
    return pl.pallas_call(
        functools.partial(_outproj_kernel, d=d),
        out_shape=(jax.ShapeDtypeStruct((dm.nt, d), F32), jax.ShapeDtypeStruct((dm.nt, d), BF16),
                   jax.ShapeDtypeStruct((dm.nt, LANE), F32)),
        grid=(dm.nt // tm,),
        in_specs=[pl.BlockSpec((tm, d), row), _resident((d, d), one), pl.BlockSpec((tm, d), row),
                  pl.BlockSpec((1, 1, 6 * d), lambda i: (cid(i), 0, 0)), pl.BlockSpec((1, d), one),
                  pl.BlockSpec((d, LANE), one), pl.BlockSpec((1, LANE), one)],
        out_specs=(pl.BlockSpec((tm, d), row), pl.BlockSpec((tm, d), row), pl.BlockSpec((tm, LANE), row)),
        compiler_params=_cparams("arbitrary"),
        name="outproj",
    )(merged, w_out, x, mod_l, g, wr, br)


def _router_kernel(lg_ref, comb_ref, *, n_exp, groups, epg):
    x = lg_ref[...]
    lane = lax.broadcasted_iota(jnp.int32, x.shape, 1).astype(F32)
    big = jnp.float32(1e9)
    neg = jnp.float32(-jnp.inf)
    is_g = (lane >= n_exp) & (lane < n_exp + groups)
    xg = jnp.where(is_g, x, neg)
    mg = jnp.max(xg, axis=-1, keepdims=True)
    sg = jnp.sum(jnp.where(is_g, jnp.exp(xg - mg), 0.0), axis=-1, keepdims=True)
    pg_top = 1.0 / sg
    gidx = jnp.min(jnp.where(xg == mg, lane, big), axis=-1, keepdims=True) - n_exp
    lo = gidx * epg
    in_grp = (lane >= lo) & (lane < lo + epg)
    xe = jnp.where(in_grp, x, neg)
    m1 = jnp.max(xe, axis=-1, keepdims=True)
    e1 = jnp.min(jnp.where(xe == m1, lane, big), axis=-1, keepdims=True)
    xe2 = jnp.where(lane == e1, neg, xe)
    m2 = jnp.max(xe2, axis=-1, keepdims=True)
    e2 = jnp.min(jnp.where(xe2 == m2, lane, big), axis=-1, keepdims=True)
    t = jnp.exp(m2 - m1)
    w1 = pg_top / (1.0 + t)
    w2 = pg_top * t / (1.0 + t)
    comb_ref[...] = jnp.where(lane == e1, w1, 0.0) + jnp.where(lane == e2, w2, 0.0)


def router(dm, logits):
    tm = dm.tm_s
    return pl.pallas_call(
        functools.partial(_router_kernel, n_exp=dm.n_exp, groups=dm.groups, epg=dm.epg),
        out_shape=jax.ShapeDtypeStruct((dm.nt, LANE), F32),
        grid=(dm.nt // tm,),
        in_specs=[pl.BlockSpec((tm, LANE), lambda i: (i, 0))],
        out_specs=pl.BlockSpec((tm, LANE), lambda i: (i, 0)),
        compiler_params=_cparams("arbitrary"),
        name="router",
    )(logits)


def _moe_dense_kernel(h_ref, comb_ref, wg_ref, wu_ref, wd_ref, x1_ref, mod_ref, o_ref, acc_ref, *, d):
    e = pl.program_id(1)

    @pl.when(e == 0)
    def _():
        acc_ref[...] = jnp.zeros_like(acc_ref)

    h = h_ref[...]
    hg = jnp.dot(h, wg_ref[0], preferred_element_type=F32)
    hu = jnp.dot(h, wu_ref[0], preferred_element_type=F32)
    comb = comb_ref[...]
    lane = lax.broadcasted_iota(jnp.int32, comb.shape, 1)
    ce = jnp.sum(jnp.where(lane == e, comb, 0.0), axis=-1, keepdims=True)
    a = (_silu(hg) * hu * ce).astype(BF16)
    acc_ref[...] += jnp.dot(a, wd_ref[0], preferred_element_type=F32)

    @pl.when(e == pl.num_programs(1) - 1)
    def _():
        g2 = mod_ref[0, :, 5 * d:6 * d]
        o_ref[...] = x1_ref[...] + g2 * acc_ref[...]


def moe_dense(dm, h2, comb, wg, wu, wd, x1, mod_l):
    tm, d, fe = dm.tm_s, dm.d, dm.fe
    cid = _cond_id(dm, tm)
    return pl.pallas_call(
        functools.partial(_moe_dense_kernel, d=d),
        out_shape=jax.ShapeDtypeStruct((dm.nt, d), F32),
        grid=(dm.nt // tm, dm.n_exp),
        in_specs=[pl.BlockSpec((tm, d), lambda i, e: (i, 0)),
                  pl.BlockSpec((tm, LANE), lambda i, e: (i, 0)),
                  pl.BlockSpec((1, d, fe), lambda i, e: (e, 0, 0)),
                  pl.BlockSpec((1, d, fe), lambda i, e: (e, 0, 0)),
                  pl.BlockSpec((1, fe, d), lambda i, e: (e, 0, 0)),
                  pl.BlockSpec((tm, d), lambda i, e: (i, 0)),
                  pl.BlockSpec((1, 1, 6 * d), lambda i, e: (cid(i), 0, 0))],
        out_specs=pl.BlockSpec((tm, d), lambda i, e: (i, 0)),
        scratch_shapes=[pltpu.VMEM((tm, d), F32)],
        compiler_params=_cparams("arbitrary", "arbitrary"),
        name="moe_dense",
    )(h2, comb, wg, wu, wd, x1, mod_l)


def _dft_tables(length):
    k = lax.broadcasted_iota(jnp.int32, (length, length), 0)
    s = lax.broadcasted_iota(jnp.int32, (length, length), 1)
    ang = ((k * s) % (2 * length)).astype(F32) * (math.pi / length)
    return jnp.cos(ang).astype(BF16), jnp.sin(ang).astype(BF16)


def _rope_tables(dm, tile):
    length = dm.dec_seq
    n_freq = dm.rope // 4
    half = dm.rope // 2
    pos = jnp.arange(length, dtype=jnp.int32)
    row = (pos // dm.grid_w).astype(F32)
    col = (pos % dm.grid_w).astype(F32)
    inv = jnp.power(ROPE_BASE, -jnp.arange(n_freq, dtype=F32) / n_freq)
    ang = jnp.concatenate([row[:, None] * inv[None, :], col[:, None] * inv[None, :]], axis=-1)
    cos, sin = jnp.cos(ang), jnp.sin(ang)
    ones = lambda n: jnp.ones((length, n), F32)
    zeros = lambda n: jnp.zeros((length, n), F32)
    tail = LANE - dm.nope - dm.rope
    cos_t = jnp.concatenate([ones(dm.nope), cos, cos, ones(tail)], axis=1)
    sa_t = jnp.concatenate([zeros(dm.nope + half), sin, zeros(tail)], axis=1)
    sb_t = jnp.concatenate([zeros(dm.nope), -sin, zeros(half + tail)], axis=1)
    ident = lambda v: jnp.full((tile, LANE), v, F32)
    return (jnp.concatenate([ident(1.0), cos_t]), jnp.concatenate([ident(0.0), sa_t]),
            jnp.concatenate([ident(0.0), sb_t]))


def _pack_w_in(dm, w_in):
    o = 0
    parts = {}
    for name, n in (("hy", 3 * dm.hw), ("cq", dm.q_lora), ("ckv", dm.kv_lora), ("kpe", dm.rope),
                    ("cf", 2 * dm.cw), ("gate", 3 * dm.d)):
        parts[name] = w_in[:, o:o + n]
        o += n
    z = lambda n: jnp.zeros((w_in.shape[0], n), w_in.dtype)
    small = [parts["cq"], parts["ckv"], z(dm.nope), parts["kpe"], z(LANE - dm.nope - dm.rope)]
    used = dm.q_lora + dm.kv_lora + LANE
    small.append(z(dm.tn - used))
    return jnp.concatenate(small + [parts["gate"], parts["hy"], parts["cf"]], axis=1).astype(BF16)


def _forward(dm, x_prompt, x_sample, cache_ckv, cache_kpe, c, c_ctx, w_mod, b_mod, norm_mix, norm_ffn,
             w_in, hy_short, hy_f_w1, hy_f_b1, hy_f_w2, hy_f_b2, hy_f_w3, hy_decay, hy_bias, w_hy_out,
             q_a_norm, w_uq, kv_a_norm, w_ukv, q_norm, k_norm, w_mla_out,
             cf_dw, cf_dw_b, cf_ln_g, cf_ln_b, w_cf_out, b_cf_out, w_out,
             w_rg, b_rg, w_re, b_re, w_eg, w_eu, w_ed):
    d, hw, cw, heads = dm.d, dm.hw, dm.cw, dm.heads
    n_ctx, nt = dm.n_ctx, dm.nt
    gate_col0, hy_col0, cf_col0 = 0, 3 * d, 3 * d + 3 * hw
    ckv_blk = dm.q_lora // dm.kv_lora
    kpe_blk = (dm.q_lora + dm.kv_lora) // LANE
    kpe_lane0 = dm.q_lora + dm.kv_lora + dm.nope

    x = jnp.concatenate([x_prompt.reshape(n_ctx, d), x_sample.reshape(dm.n_lat, d)], axis=0)
    cond8 = jnp.concatenate([c_ctx[None, :], c, jnp.zeros((8 - 1 - dm.dec_batch, d), F32)], axis=0)
    mod = adaln(dm, cond8, w_mod, b_mod)

    dft = {dm.seq: _dft_tables(dm.seq), dm.dec_seq: _dft_tables(dm.dec_seq)}
    cos_t, sa_t, sb_t = _rope_tables(dm, dm.tm_s)
    scale = float(dm.dk) ** -0.5
    tail = LANE - dm.dk
    tc_ctx = min(hw, 1024)
    tq_ctx = dm.seq
    pairs_ctx = heads // 2
    tm_cache = min(dm.dec_batch * dm.past, dm.tm_s)

    ckv_out, kpe_out = [], []
    for l in range(dm.depth):
        mod_l = mod[l][:, None, :]
        w_packed = _pack_w_in(dm, w_in[l])
        p2, p = inproj(dm, x, mod_l, norm_mix[l][None, :], w_packed)

        w1p = jnp.zeros((LANE, dm.hy_fhid), F32).at[:1 + 2 * dm.hy_freqs].set(hy_f_w1[l])
        z2 = None
        for (length, row0, nseq, tc) in ((dm.seq, 0, dm.batch, tc_ctx),
                                         (dm.dec_seq, n_ctx, dm.dec_batch, min(hw, dm.tc_hy))):
            cmat, smat = dft[length]
            filt = hyena_filters(dm, length, w1p, hy_f_b1[l][None, :], hy_f_w2[l], hy_f_b2[l][None, :],
                                 hy_f_w3[l], hy_decay[l][None, :])
            kc, ks, kn = hyena_spectra(dm, length, filt, cmat, smat)
            z2 = hyena_conv(dm, p, hy_col0, row0, nseq, length, tc, hy_short[l], hy_bias[l], kc, ks, kn,
                            cmat, smat, prev=z2)

        w_uq_p = jnp.pad(w_uq[l].reshape(dm.q_lora, heads, dm.dk), ((0, 0), (0, 0), (0, tail)))
        w_uq_p = w_uq_p.reshape(dm.q_lora, heads * LANE).astype(BF16)
        wkv = w_ukv[l].reshape(dm.kv_lora, heads, dm.nope + dm.vh)
        wk_p = jnp.pad(wkv[:, :, :dm.nope], ((0, 0), (0, 0), (0, LANE - dm.nope)))
        w_ukv_p = jnp.concatenate([wk_p.reshape(dm.kv_lora, heads * LANE),
                                   wkv[:, :, dm.nope:].reshape(dm.kv_lora, heads * dm.vh)], axis=1).astype(BF16)
        gq = jnp.concatenate([q_norm[l] * scale, jnp.zeros((tail,), F32)])[None, :]
        gk = jnp.concatenate([k_norm[l], jnp.zeros((tail,), F32)])[None, :]
        q = mla_q(dm, p2, q_a_norm[l][None, :], w_uq_p, gq, cos_t, sa_t, sb_t)
        k, v, ckv_n = mla_kv(dm, p2, ckv_blk, p2, kpe_blk, nt, dm.tm_s, kv_a_norm[l][None, :], w_ukv_p, gk,
                             cos_t, sa_t, sb_t, _rope_blk(dm, dm.tm_s), True)
        kpe_c = jnp.pad(cache_kpe[:, l].reshape(dm.dec_batch * dm.past, dm.rope),
                        ((0, 0), (dm.nope, LANE - dm.dk)))
        k_c, v_c, _ = mla_kv(dm, cache_ckv[:, l].reshape(dm.dec_batch * dm.past, dm.kv_lora), 0, kpe_c, 0,
                             dm.dec_batch * dm.past, tm_cache, kv_a_norm[l][None, :], w_ukv_p, gk,
                             cos_t, sa_t, sb_t, lambda i: 0, False)
        o = attention(dm, q, [(k, v, dm.seq, 0)], 0, dm.batch, dm.seq, tq_ctx, pairs_ctx)
        o = attention(dm, q, [(k, v, dm.dec_seq, n_ctx), (k_c, v_c, dm.past, 0)], n_ctx, dm.dec_batch,
                      dm.dec_seq, min(dm.tq, dm.dec_seq), 1, prev=o)
        ckv_out.append(ckv_n[:n_ctx].reshape(dm.batch, dm.seq, dm.kv_lora))
        kpe_out.append(p2[:n_ctx, kpe_lane0:kpe_lane0 + dm.rope].reshape(dm.batch, dm.seq, dm.rope))

        hc = conformer(dm, p, cf_col0, 0, dm.batch, dm.seq, cf_dw[l], cf_dw_b[l][None, :],
                       cf_ln_g[l][None, :], cf_ln_b[l][None, :])
        hc = conformer(dm, p, cf_col0, n_ctx, dm.dec_batch, dm.dec_seq, cf_dw[l], cf_dw_b[l][None, :],
                       cf_ln_g[l][None, :], cf_ln_b[l][None, :], prev=hc)

        merged = merge(dm, z2, o, hc, p, gate_col0, w_hy_out[l].astype(BF16), w_mla_out[l].astype(BF16),
                       w_cf_out[l].astype(BF16), b_cf_out[l][None, :])
        wr = jnp.concatenate([w_re[l], w_rg[l], jnp.zeros((d, LANE - dm.n_exp - dm.groups), F32)],
                             axis=1).astype(BF16)
        br = jnp.concatenate([b_re[l], b_rg[l], jnp.zeros((LANE - dm.n_exp - dm.groups,), F32)])[None, :]
        x1, h2, logits = outproj(dm, merged, w_out[l].astype(BF16), x, mod_l, norm_ffn[l][None, :], wr, br)
        comb = router(dm, logits)
        x = moe_dense(dm, h2, comb, w_eg[l].astype(BF16), w_eu[l].astype(BF16), w_ed[l].astype(BF16),
                      x1, mod_l)

    y_prompt = x[:n_ctx].reshape(dm.batch, dm.seq, d)
    y_sample = x[n_ctx:].reshape(dm.dec_batch, dm.dec_seq, d)
    return y_prompt, y_sample, jnp.stack(ckv_out, axis=1), jnp.stack(kpe_out, axis=1)


def kernel(x_prompt, x_sample, cache_ckv, cache_kpe, c, c_ctx, w_mod, b_mod, norm_mix, norm_ffn, w_in, hy_short, hy_f_w1, hy_f_b1, hy_f_w2, hy_f_b2, hy_f_w3, hy_decay, hy_bias, w_hy_out, q_a_norm, w_uq, kv_a_norm, w_ukv, q_norm, k_norm, w_mla_out, cf_dw, cf_dw_b, cf_ln_g, cf_ln_b, w_cf_out, b_cf_out, w_out, w_rg, b_rg, w_re, b_re, w_eg, w_eu, w_ed):
    return _forward(Dims(), x_prompt, x_sample, cache_ckv, cache_kpe, c, c_ctx, w_mod, b_mod, norm_mix,
                    norm_ffn, w_in, hy_short, hy_f_w1, hy_f_b1, hy_f_w2, hy_f_b2, hy_f_w3, hy_decay, hy_bias,
                    w_hy_out, q_a_norm, w_uq, kv_a_norm, w_ukv, q_norm, k_norm, w_mla_out,
                    cf_dw, cf_dw_b, cf_ln_g, cf_ln_b, w_cf_out, b_cf_out, w_out,
                    w_rg, b_rg, w_re, b_re, w_eg, w_eu, w_ed)
```

```python
import functools
import math
from typing import NamedTuple

import jax
import jax.numpy as jnp
from jax import lax
from jax.experimental import pallas as pl
from jax.experimental.pallas import tpu as pltpu

F32 = jnp.float32
BF16 = jnp.bfloat16
U32 = jnp.uint32
EPS = 1e-6
LANE = 128
SUBLANE = 8
CONV_SUB = 256
VMEM_LIMIT_BYTES = 56 << 20
ROPE_BASE = 10000.0
HY_ORDER = 2
HY_SHORT = 3
HALO = 16
CONV_CHUNK = 16


class Dims(NamedTuple):
    d: int = 2048
    batch: int = 32
    seq: int = 256
    depth: int = 4
    dec_batch: int = 4
    dec_seq: int = 2048
    past: int = 256
    grid_w: int = 64
    hw: int = 1024
    hy_freqs: int = 16
    hy_fhid: int = 64
    heads: int = 16
    nope: int = 64
    rope: int = 32
    vh: int = 64
    q_lora: int = 512
    kv_lora: int = 256
    cw: int = 1024
    cf_k: int = 31
    groups: int = 4
    epg: int = 4
    fe: int = 512
    tm: int = 1024
    tm_s: int = 512
    tn: int = 1024
    tq: int = 512
    tc_hy: int = 256
    moe_tile: int = 256

    @property
    def n_ctx(self):
        return self.batch * self.seq

    @property
    def n_lat(self):
        return self.dec_batch * self.dec_seq

    @property
    def nt(self):
        return self.n_ctx + self.n_lat

    @property
    def n_exp(self):
        return self.groups * self.epg

    @property
    def dk(self):
        return self.nope + self.rope


def _cparams(*sem):
    return pltpu.CompilerParams(dimension_semantics=sem, vmem_limit_bytes=VMEM_LIMIT_BYTES)


def _resident(shape, index_map):
    return pl.BlockSpec(shape, index_map, pipeline_mode=pl.Buffered(1))


def _silu(x):
    return x * jax.nn.sigmoid(x)


def _pack_bf16_pairs(x):
    n = x.shape[1] // 2
    lo = lax.bitcast_convert_type(x[:, :n].astype(jnp.bfloat16).astype(F32), U32) >> 16
    hi = lax.bitcast_convert_type(x[:, n:].astype(jnp.bfloat16).astype(F32), U32) & jnp.uint32(0xFFFF0000)
    return hi | lo


def _unpack_bf16_pairs(u):
    lo = lax.bitcast_convert_type(u << 16, F32)
    hi = lax.bitcast_convert_type(u & jnp.uint32(0xFFFF0000), F32)
    return jnp.concatenate([lo, hi], axis=1)


def _rms(x, g):
    return x * lax.rsqrt(jnp.mean(x * x, axis=-1, keepdims=True) + EPS) * g


def _cond_id(dm, tile):
    n_ctx_tiles = dm.n_ctx // tile
    per_seq = dm.dec_seq // tile
    return lambda i: jnp.where(i < n_ctx_tiles, 0, 1 + (i - n_ctx_tiles) // per_seq)


def _rope_blk(dm, tile):
    n_ctx_tiles = dm.n_ctx // tile
    per_seq = dm.dec_seq // tile
    return lambda i: jnp.where(i < n_ctx_tiles, 0, 1 + (i - n_ctx_tiles) % per_seq)


def _adaln_kernel(c_ref, w_ref, b_ref, o_ref):
    a = _silu(c_ref[...]).astype(BF16)
    o_ref[0] = jnp.dot(a, w_ref[0].astype(BF16), preferred_element_type=F32) + b_ref[0]


def adaln(dm, cond8, w_mod, b_mod):
    depth, d, n = w_mod.shape
    tn = min(n, 1024)
    return pl.pallas_call(
        _adaln_kernel,
        out_shape=jax.ShapeDtypeStruct((depth, 8, n), F32),
        grid=(depth, n // tn),
        in_specs=[pl.BlockSpec((8, d), lambda l, j: (0, 0)),
                  pl.BlockSpec((1, d, tn), lambda l, j: (l, 0, j)),
                  pl.BlockSpec((1, 1, tn), lambda l, j: (l, 0, j))],
        out_specs=pl.BlockSpec((1, 8, tn), lambda l, j: (l, 0, j)),
        compiler_params=_cparams("arbitrary", "arbitrary"),
        name="adaln",
    )(cond8, w_mod, b_mod.reshape(depth, 1, n))


def _inproj_kernel(x_ref, mod_ref, g_ref, w_ref, p2_ref, p_ref, hn_ref, *, d):
    j = pl.program_id(1)

    @pl.when(j == 0)
    def _():
        y = _rms(x_ref[...], g_ref[...])
        sh = mod_ref[0, :, 0:d]
        sc = mod_ref[0, :, d:2 * d]
        hn_ref[...] = (y * (1.0 + sc) + sh).astype(BF16)

    acc = jnp.dot(hn_ref[...], w_ref[...], preferred_element_type=F32)
    p_ref[...] = acc.astype(BF16)

    @pl.when(j == 0)
    def _():
        p2_ref[...] = acc


def inproj(dm, x, mod_l, g, w_packed):
    nt, d = x.shape
    tm, tn = dm.tm, dm.tn
    ncols = w_packed.shape[1]
    nj = ncols // tn
    cid = _cond_id(dm, tm)
    return pl.pallas_call(
        functools.partial(_inproj_kernel, d=d),
        out_shape=(jax.ShapeDtypeStruct((nt, tn), F32),
                   jax.ShapeDtypeStruct((nt, ncols - tn), BF16)),
        grid=(nt // tm, nj),
        in_specs=[pl.BlockSpec((tm, d), lambda i, j: (i, 0)),
                  pl.BlockSpec((1, 1, 6 * d), lambda i, j: (cid(i), 0, 0)),
                  pl.BlockSpec((1, d), lambda i, j: (0, 0)),
                  pl.BlockSpec((d, tn), lambda i, j: (0, j))],
        out_specs=(pl.BlockSpec((tm, tn), lambda i, j: (i, 0)),
                   pl.BlockSpec((tm, tn), lambda i, j: (i, jnp.maximum(j - 1, 0)))),
        scratch_shapes=[pltpu.VMEM((tm, d), BF16)],
        compiler_params=_cparams("arbitrary", "arbitrary"),
        name="inproj",
    )(x, mod_l, g, w_packed)


def _dot3(a, b):
    ah = a.astype(BF16)
    al = (a - ah.astype(F32)).astype(BF16)
    bh = b.astype(BF16)
    bl = (b - bh.astype(F32)).astype(BF16)
    return (jnp.dot(ah, bh, preferred_element_type=F32) + jnp.dot(al, bh, preferred_element_type=F32)
            + jnp.dot(ah, bl, preferred_element_type=F32))


def _hyfilt_kernel(w1_ref, b1_ref, w2_ref, b2_ref, w3_ref, dec_ref, f_ref, *, length, nfreq):
    row = lax.broadcasted_iota(jnp.int32, (length, LANE), 0).astype(F32)
    lane = lax.broadcasted_iota(jnp.int32, (length, LANE), 1)
    tn = row / float(length)
    fr = jnp.where(lane <= nfreq, lane, lane - nfreq).astype(F32)
    ang = (2.0 * math.pi) * tn * fr
    feat = jnp.where(lane == 0, tn, jnp.where(lane <= nfreq, jnp.cos(ang),
                                              jnp.where(lane <= 2 * nfreq, jnp.sin(ang), 0.0)))
    h = jnp.sin(_dot3(feat, w1_ref[...]) + b1_ref[...])
    h = jnp.sin(_dot3(h, w2_ref[...]) + b2_ref[...])
    f_ref[...] = _dot3(h, w3_ref[...]) * jnp.exp(-tn[:, 0:1] * dec_ref[...])


def hyena_filters(dm, length, w1p, b1, w2, b2, w3, decay):
    ncol = w3.shape[1]
    tn = min(ncol, 1024)
    fh = w2.shape[0]
    return pl.pallas_call(
        functools.partial(_hyfilt_kernel, length=length, nfreq=dm.hy_freqs),
        out_shape=jax.ShapeDtypeStruct((length, ncol), F32),
        grid=(ncol // tn,),
        in_specs=[pl.BlockSpec((LANE, fh), lambda j: (0, 0)),
                  pl.BlockSpec((1, fh), lambda j: (0, 0)),
                  pl.BlockSpec((fh, fh), lambda j: (0, 0)),
                  pl.BlockSpec((1, fh), lambda j: (0, 0)),
                  pl.BlockSpec((fh, tn), lambda j: (0, j)),
                  pl.BlockSpec((1, tn), lambda j: (0, j))],
        out_specs=pl.BlockSpec((length, tn), lambda j: (0, j)),
        compiler_params=_cparams("arbitrary"),
        name="hyena_filters",
    )(w1p, b1, w2, b2, w3, decay)


def _hyspec_kernel(hf_ref, hb_ref, c_ref, s_ref, kc_ref, ks_ref, kn_ref, *, length):
    row = lax.broadcasted_iota(jnp.int32, (length, 1), 0)
    hf = hf_ref[...]
    hb0 = jnp.where(row == 0, 0.0, hb_ref[...])
    sp = hf + hb0
    sm = hf - hb0
    inv = 1.0 / length
    kc = jnp.dot(c_ref[...], sp.astype(BF16), preferred_element_type=F32)
    kc_ref[0] = kc * jnp.where(row == 0, 0.5 * inv, inv)
    ks_ref[0] = jnp.dot(s_ref[...], sm.astype(BF16), preferred_element_type=F32) * inv
    sgn = (1 - 2 * (row & 1)).astype(F32)
    kn = jnp.sum(sgn * sp, axis=0, keepdims=True) * (0.5 * inv)
    kn_ref[0] = jnp.broadcast_to(kn, kn_ref.shape[1:])


def hyena_spectra(dm, length, filt, cmat, smat):
    hw = dm.hw
    tc = min(hw, 512)
    nct = hw // tc
    shp = jax.ShapeDtypeStruct((HY_ORDER, length, hw), F32)
    return pl.pallas_call(
        functools.partial(_hyspec_kernel, length=length),
        out_shape=(shp, shp, jax.ShapeDtypeStruct((HY_ORDER, 8, hw), F32)),
        grid=(HY_ORDER, nct),
        in_specs=[pl.BlockSpec((length, tc), lambda n, j: (0, (2 * n) * nct + j)),
                  pl.BlockSpec((length, tc), lambda n, j: (0, (2 * n + 1) * nct + j)),
                  _resident((length, length), lambda n, j: (0, 0)),
                  _resident((length, length), lambda n, j: (0, 0))],
        out_specs=(pl.BlockSpec((1, length, tc), lambda n, j: (n, 0, j)),
                   pl.BlockSpec((1, length, tc), lambda n, j: (n, 0, j)),
                   pl.BlockSpec((1, 8, tc), lambda n, j: (n, 0, j))),
        compiler_params=_cparams("arbitrary", "arbitrary"),
        name="hyena_spectra",
    )(filt, filt, cmat, smat)


def _hyconv_kernel(v_ref, x1_ref, x2_ref, swv_ref, sw1_ref, sw2_ref, bias_ref, kc_ref, ks_ref, kn_ref,
                   c_ref, s_ref, *rest, length):
    o_ref, zc_ref, zs_ref = rest[-3:]
    row = lax.broadcasted_iota(jnp.int32, (length, 1), 0)
    sgn = (1 - 2 * (row & 1)).astype(F32)

    def short(x_ref, w_ref):
        x = x_ref[...].astype(F32)
        xp = jnp.where(row == 0, 0.0, pltpu.roll(x, 1, 0))
        xn = jnp.where(row == length - 1, 0.0, pltpu.roll(x, length - 1, 0))
        return w_ref[0:1, :] * xp + w_ref[1:2, :] * x + w_ref[2:3, :] * xn

    def longconv(z, n):
        zb = z.astype(BF16)
        uc = jnp.dot(c_ref[...], zb, preferred_element_type=F32)
        us = jnp.dot(s_ref[...], zb, preferred_element_type=F32)
        un = jnp.sum(sgn * z, axis=0, keepdims=True)
        kc = kc_ref[n]
        ks = ks_ref[n]
        zc_ref[...] = (uc * kc - us * ks).astype(BF16)
        zs_ref[...] = (uc * ks + us * kc).astype(BF16)
        y = jnp.dot(c_ref[...], zc_ref[...], preferred_element_type=F32)
        y = y + jnp.dot(s_ref[...], zs_ref[...], preferred_element_type=F32)
        return y + sgn * (un * kn_ref[n, 0:1, :])

    v = short(v_ref, swv_ref)
    z = short(x1_ref, sw1_ref) * (longconv(v, 0) + v * bias_ref[0:1, :])
    z = short(x2_ref, sw2_ref) * (longconv(z, 1) + z * bias_ref[1:2, :])
    o_ref[...] = z.astype(BF16)


def hyena_conv(dm, p, hy_col0, row0, nseq, length, tc, short_w, bias, kc, ks, kn, cmat, smat, prev=None):
    hw = dm.hw
    nct = hw // tc
    rb0 = row0 // length
    cb0 = hy_col0 // tc

    def part(k):
        return pl.BlockSpec((length, tc), lambda j, s, k=k: (rb0 + s, cb0 + k * nct + j))

    def sw(k):
        return pl.BlockSpec((HY_SHORT, tc), lambda j, s, k=k: (0, k * nct + j))

    in_specs = [part(0), part(1), part(2), sw(0), sw(1), sw(2),
                pl.BlockSpec((HY_ORDER, tc), lambda j, s: (0, j)),
                _resident((HY_ORDER, length, tc), lambda j, s: (0, 0, j)),
                _resident((HY_ORDER, length, tc), lambda j, s: (0, 0, j)),
                pl.BlockSpec((HY_ORDER, 8, tc), lambda j, s: (0, 0, j)),
                _resident((length, length), lambda j, s: (0, 0)),
                _resident((length, length), lambda j, s: (0, 0))]
    args = [p, p, p, short_w, short_w, short_w, bias, kc, ks, kn, cmat, smat]
    aliases = {}
    if prev is not None:
        in_specs.append(pl.BlockSpec(memory_space=pl.ANY))
        args.append(prev)
        aliases = {len(args) - 1: 0}
    return pl.pallas_call(
        functools.partial(_hyconv_kernel, length=length),
        out_shape=jax.ShapeDtypeStruct((dm.nt, hw), BF16),
        grid=(nct, nseq),
        in_specs=in_specs,
        out_specs=pl.BlockSpec((length, tc), lambda j, s: (rb0 + s, j)),
        scratch_shapes=[pltpu.VMEM((length, tc), BF16), pltpu.VMEM((length, tc), BF16)],
        input_output_aliases=aliases,
        compiler_params=_cparams("arbitrary", "arbitrary"),
        name="hyena_conv_%d" % length,
    )(*args)


def _head_norm_rope(xh, gain, cos, sa, sb, inv_dk, half):
    ss = jnp.sum(xh * xh, axis=-1, keepdims=True) * inv_dk
    xh = xh * lax.rsqrt(ss + EPS) * gain
    return xh * cos + pltpu.roll(xh, half, 1) * sa + pltpu.roll(xh, LANE - half, 1) * sb


def _q_kernel(cq_ref, ga_ref, w_ref, gh_ref, cos_ref, sa_ref, sb_ref, o_ref, *, heads, dk, half):
    y = _rms(cq_ref[...], ga_ref[...])
    q = jnp.dot(y.astype(BF16), w_ref[...], preferred_element_type=F32)
    cos, sa, sb, gh = cos_ref[...], sa_ref[...], sb_ref[...], gh_ref[...]
    for h in range(heads):
        qh = _head_norm_rope(q[:, h * LANE:(h + 1) * LANE], gh, cos, sa, sb, 1.0 / dk, half)
        o_ref[:, h * LANE:(h + 1) * LANE] = qh.astype(BF16)


def mla_q(dm, p2, ga, w_uq, gh, cos_t, sa_t, sb_t):
    tm = dm.tm_s
    rb = _rope_blk(dm, tm)
    tab = pl.BlockSpec((tm, LANE), lambda i: (rb(i), 0))
    return pl.pallas_call(
        functools.partial(_q_kernel, heads=dm.heads, dk=dm.dk, half=dm.rope // 2),
        out_shape=jax.ShapeDtypeStruct((dm.nt, dm.heads * LANE), BF16),
        grid=(dm.nt // tm,),
        in_specs=[pl.BlockSpec((tm, dm.q_lora), lambda i: (i, 0)),
                  pl.BlockSpec((1, dm.q_lora), lambda i: (0, 0)),
                  pl.BlockSpec((dm.q_lora, dm.heads * LANE), lambda i: (0, 0)),
                  pl.BlockSpec((1, LANE), lambda i: (0, 0)),
                  tab, tab, tab],
        out_specs=pl.BlockSpec((tm, dm.heads * LANE), lambda i: (i, 0)),
        compiler_params=_cparams("arbitrary"),
        name="mla_q",
    )(p2, ga, w_uq, gh, cos_t, sa_t, sb_t)


def _kv_kernel(ckv_ref, kpe_ref, ga_ref, w_ref, gk_ref, cos_ref, sa_ref, sb_ref, k_ref, v_ref, cn_ref,
               *, heads, dk, half, normalize):
    x = ckv_ref[...]
    if normalize:
        x = _rms(x, ga_ref[...])
    cn_ref[...] = x
    kv = jnp.dot(x.astype(BF16), w_ref[...], preferred_element_type=F32)
    kpe = kpe_ref[...]
    cos, sa, sb, gk = cos_ref[...], sa_ref[...], sb_ref[...], gk_ref[...]
    for h in range(heads):
        kh = _head_norm_rope(kv[:, h * LANE:(h + 1) * LANE] + kpe, gk, cos, sa, sb, 1.0 / dk, half)
        k_ref[:, h * LANE:(h + 1) * LANE] = kh.astype(BF16)
    v_ref[...] = kv[:, heads * LANE:].astype(BF16)


def mla_kv(dm, ckv_src, ckv_blk, kpe_src, kpe_blk, nrows, tm, ga, w_ukv, gk, cos_t, sa_t, sb_t, rope_blk,
           normalize):
    hk = dm.heads * LANE
    hv = dm.heads * dm.vh
    tab = pl.BlockSpec((tm, LANE), lambda i: (rope_blk(i), 0))
    return pl.pallas_call(
        functools.partial(_kv_kernel, heads=dm.heads, dk=dm.dk, half=dm.rope // 2, normalize=normalize),
        out_shape=(jax.ShapeDtypeStruct((nrows, hk), BF16),
                   jax.ShapeDtypeStruct((nrows, hv), BF16),
                   jax.ShapeDtypeStruct((nrows, dm.kv_lora), F32)),
        grid=(nrows // tm,),
        in_specs=[pl.BlockSpec((tm, dm.kv_lora), lambda i: (i, ckv_blk)),
                  pl.BlockSpec((tm, LANE), lambda i: (i, kpe_blk)),
                  pl.BlockSpec((1, dm.kv_lora), lambda i: (0, 0)),
                  pl.BlockSpec((dm.kv_lora, hk + hv), lambda i: (0, 0)),
                  pl.BlockSpec((1, LANE), lambda i: (0, 0)),
                  tab, tab, tab],
        out_specs=(pl.BlockSpec((tm, hk), lambda i: (i, 0)),
                   pl.BlockSpec((tm, hv), lambda i: (i, 0)),
                   pl.BlockSpec((tm, dm.kv_lora), lambda i: (i, 0))),
        compiler_params=_cparams("arbitrary"),
        name="mla_kv_norm" if normalize else "mla_kv_cache",
    )(ckv_src, kpe_src, ga, w_ukv, gk, cos_t, sa_t, sb_t)


def _attn_kernel(q_ref, *refs, nseg, pairs, vh):
    o_ref = refs[-1]
    nt_dims = (((1,), (1,)), ((), ()))
    lane = lax.broadcasted_iota(jnp.int32, (q_ref.shape[0], LANE), 1)
    for p in range(pairs):
        outs = []
        for hh in range(2):
            c0 = (2 * p + hh) * LANE
            q = q_ref[:, c0:c0 + LANE]
            s = [lax.dot_general(q, refs[2 * i][:, c0:c0 + LANE], nt_dims, preferred_element_type=F32)
                 for i in range(nseg)]
            m = functools.reduce(jnp.maximum, [jnp.max(si, axis=-1, keepdims=True) for si in s])
            e = [jnp.exp(si - m) for si in s]
            l = functools.reduce(jnp.add, [jnp.sum(ei, axis=-1, keepdims=True) for ei in e])
            acc = functools.reduce(jnp.add, [
                jnp.dot(e[i].astype(BF16), refs[2 * i + 1][:, p * LANE:(p + 1) * LANE],
                        preferred_element_type=F32) for i in range(nseg)])
            outs.append(acc / l)
        o_ref[:, p * LANE:(p + 1) * LANE] = jnp.where(lane < vh, outs[0], outs[1]).astype(BF16)


def attention(dm, q, segs, row0, nseq, length, tq, pairs, prev=None):
    npg = dm.heads // 2 // pairs
    nqt = length // tq
    qb0 = row0 // tq
    in_specs = [pl.BlockSpec((tq, pairs * 2 * LANE), lambda s, g, t: (qb0 + s * nqt + t, g))]
    args = [q]
    for (k, v, lk, r0) in segs:
        kb0 = r0 // lk
        in_specs.append(pl.BlockSpec((lk, pairs * 2 * LANE), lambda s, g, t, kb0=kb0: (kb0 + s, g)))
        in_specs.append(pl.BlockSpec((lk, pairs * LANE), lambda s, g, t, kb0=kb0: (kb0 + s, g)))
        args += [k, v]
    aliases = {}
    if prev is not None:
        in_specs.append(pl.BlockSpec(memory_space=pl.ANY))
        args.append(prev)
        aliases = {len(args) - 1: 0}
    return pl.pallas_call(
        functools.partial(_attn_kernel, nseg=len(segs), pairs=pairs, vh=dm.vh),
        out_shape=jax.ShapeDtypeStruct((dm.nt, dm.heads * dm.vh), BF16),
        grid=(nseq, npg, nqt),
        in_specs=in_specs,
        out_specs=pl.BlockSpec((tq, pairs * LANE), lambda s, g, t: (qb0 + s * nqt + t, g)),
        input_output_aliases=aliases,
        compiler_params=_cparams("arbitrary", "arbitrary", "arbitrary"),
        name="attention_%d" % length,
    )(*args)


def _conf_kernel(a_ref, g_ref, w_ref, b_ref, lg_ref, lb_ref, *rest, length, ktaps, sub):
    o_ref, hs_ref, sh_ref = rest[-3:]
    cw = a_ref.shape[1]
    off = HALO - (ktaps - 1) // 2
    span = sub + SUBLANE * ((off + ktaps - 1) // SUBLANE)
    hs_ref[0:HALO, :] = jnp.zeros((HALO, cw), F32)
    hs_ref[HALO + length:, :] = jnp.zeros((HALO, cw), F32)
    hs_ref[HALO:HALO + length, :] = a_ref[...].astype(F32) * jax.nn.sigmoid(g_ref[...].astype(F32))
    bias, lg, lb = b_ref[...], lg_ref[...], lb_ref[...]

    for sb in range(length // sub):
        base = sb * sub
        for r in range(SUBLANE):
            sh_ref[r] = hs_ref[base + r:base + r + span, :]

        def body(c, carry):
            t0 = pl.multiple_of(c * CONV_CHUNK, CONV_CHUNK)
            acc = jnp.broadcast_to(bias, (CONV_CHUNK, cw))
            for j in range(ktaps):
                q, r = divmod(off + j, SUBLANE)
                acc = acc + w_ref[j:j + 1, :] * sh_ref[r, pl.ds(t0 + SUBLANE * q, CONV_CHUNK), :]
            mu = jnp.mean(acc, axis=-1, keepdims=True)
            xc = acc - mu
            var = jnp.mean(xc * xc, axis=-1, keepdims=True)
            y = xc * lax.rsqrt(var + EPS) * lg + lb
            o_ref[pl.ds(base + t0, CONV_CHUNK), :] = _silu(y).astype(BF16)
            return carry

        lax.fori_loop(0, sub // CONV_CHUNK, body, 0)


def conformer(dm, p, cf_col0, row0, nseq, length, w, b, lg, lb, prev=None):
    cw = dm.cw
    rb0 = row0 // length
    cb0 = cf_col0 // cw
    one = lambda s: (0, 0)
    in_specs = [pl.BlockSpec((length, cw), lambda s: (rb0 + s, cb0)),
                pl.BlockSpec((length, cw), lambda s: (rb0 + s, cb0 + 1)),
                pl.BlockSpec((dm.cf_k, cw), one), pl.BlockSpec((1, cw), one),
                pl.BlockSpec((1, cw), one), pl.BlockSpec((1, cw), one)]
    args = [p, p, w, b, lg, lb]
    aliases = {}
    if prev is not None:
        in_specs.append(pl.BlockSpec(memory_space=pl.ANY))
        args.append(prev)
        aliases = {len(args) - 1: 0}
    sub = min(length, CONV_SUB)
    span = sub + SUBLANE * ((HALO + (dm.cf_k - 1) // 2) // SUBLANE)
    return pl.pallas_call(
        functools.partial(_conf_kernel, length=length, ktaps=dm.cf_k, sub=sub),
        out_shape=jax.ShapeDtypeStruct((dm.nt, cw), BF16),
        grid=(nseq,),
        in_specs=in_specs,
        out_specs=pl.BlockSpec((length, cw), lambda s: (rb0 + s, 0)),
        scratch_shapes=[pltpu.VMEM((length + 2 * HALO, cw), F32),
                        pltpu.VMEM((SUBLANE, span, cw), F32)],
        input_output_aliases=aliases,
        compiler_params=_cparams("arbitrary"),
        name="conformer_%d" % length,
    )(*args)


def _merge_kernel(z_ref, o_ref, h_ref, ga_ref, gb_ref, gc_ref, wa_ref, wb_ref, wc_ref, bc_ref, m_ref):
    ya = jnp.dot(z_ref[...], wa_ref[...], preferred_element_type=F32)
    yb = jnp.dot(o_ref[...], wb_ref[...], preferred_element_type=F32)
    yc = jnp.dot(h_ref[...], wc_ref[...], preferred_element_type=F32) + bc_ref[...]
    sg = lambda r: jax.nn.sigmoid(r[...].astype(F32))
    m_ref[...] = (sg(ga_ref) * ya + sg(gb_ref) * yb + sg(gc_ref) * yc).astype(BF16)


def merge(dm, z2, o, hc, p, gate_col0, wa, wb, wc, bc):
    tm, d = dm.tm_s, dm.d
    gb0 = gate_col0 // d
    row = lambda i: (i, 0)
    one = lambda i: (0, 0)
    return pl.pallas_call(
        _merge_kernel,
        out_shape=jax.ShapeDtypeStruct((dm.nt, d), BF16),
        grid=(dm.nt // tm,),
        in_specs=[pl.BlockSpec((tm, dm.hw), row), pl.BlockSpec((tm, dm.heads * dm.vh), row),
                  pl.BlockSpec((tm, dm.cw), row),
                  pl.BlockSpec((tm, d), lambda i: (i, gb0)), pl.BlockSpec((tm, d), lambda i: (i, gb0 + 1)),
                  pl.BlockSpec((tm, d), lambda i: (i, gb0 + 2)),
                  _resident((dm.hw, d), one), _resident((dm.heads * dm.vh, d), one),
                  _resident((dm.cw, d), one), pl.BlockSpec((1, d), one)],
        out_specs=pl.BlockSpec((tm, d), row),
        compiler_params=_cparams("arbitrary"),
        name="merge",
    )(z2, o, hc, p, p, p, wa, wb, wc, bc)


def _outproj_kernel(m_ref, w_ref, x_ref, mod_ref, g_ref, wr_ref, br_ref, x1_ref, h2_ref, lg_ref, *, d):
    y = jnp.dot(m_ref[...], w_ref[...], preferred_element_type=F32)
    g1 = mod_ref[0, :, 2 * d:3 * d]
    sh2 = mod_ref[0, :, 3 * d:4 * d]
    sc2 = mod_ref[0, :, 4 * d:5 * d]
    x1 = x_ref[...] + g1 * y
    x1_ref[...] = x1
    h2 = _rms(x1, g_ref[...]) * (1.0 + sc2) + sh2
    h2_ref[...] = _pack_bf16_pairs(h2)
    lg_ref[...] = jnp.dot(h2.astype(BF16), wr_ref[...], preferred_element_type=F32) + br_ref[...]


def outproj(dm, merged, w_out, x, mod_l, g, wr, br):
    tm, d = dm.tm_s, dm.d
    cid = _cond_id(dm, tm)
    row = lambda i: (i, 0)
    one = lambda i: (0, 0)
    return pl.pallas_call(
        functools.partial(_outproj_kernel, d=d),
        out_shape=(jax.ShapeDtypeStruct((dm.nt, d), F32), jax.ShapeDtypeStruct((dm.nt, d // 2), U32),
                   jax.ShapeDtypeStruct((dm.nt, LANE), F32)),
        grid=(dm.nt // tm,),
        in_specs=[pl.BlockSpec((tm, d), row), _resident((d, d), one), pl.BlockSpec((tm, d), row),
                  pl.BlockSpec((1, 1, 6 * d), lambda i: (cid(i), 0, 0)), pl.BlockSpec((1, d), one),
                  pl.BlockSpec((d, LANE), one), pl.BlockSpec((1, LANE), one)],
        out_specs=(pl.BlockSpec((tm, d), row), pl.BlockSpec((tm, d // 2), row), pl.BlockSpec((tm, LANE), row)),
        compiler_params=_cparams("arbitrary"),
        name="outproj",
    )(merged, w_out, x, mod_l, g, wr, br)


def _route_kernel(lg_ref, info_ref, cnt_ref, carry_ref, *, n_exp, groups, epg):
    @pl.when(pl.program_id(0) == 0)
    def _():
        carry_ref[...] = jnp.zeros_like(carry_ref)

    x = lg_ref[...]
    lane = lax.broadcasted_iota(jnp.int32, x.shape, 1).astype(F32)
    big = jnp.float32(1e9)
    neg = jnp.float32(-jnp.inf)
    is_g = (lane >= n_exp) & (lane < n_exp + groups)
    xg = jnp.where(is_g, x, neg)
    mg = jnp.max(xg, axis=-1, keepdims=True)
    sg = jnp.sum(jnp.where(is_g, jnp.exp(xg - mg), 0.0), axis=-1, keepdims=True)
    pg_top = 1.0 / sg
    gidx = jnp.min(jnp.where(xg == mg, lane, big), axis=-1, keepdims=True) - n_exp
    lo = gidx * epg
    in_grp = (lane >= lo) & (lane < lo + epg)
    xe = jnp.where(in_grp, x, neg)
    m1 = jnp.max(xe, axis=-1, keepdims=True)
    e1 = jnp.min(jnp.where(xe == m1, lane, big), axis=-1, keepdims=True)
    xe2 = jnp.where(lane == e1, neg, xe)
    m2 = jnp.max(xe2, axis=-1, keepdims=True)
    e2 = jnp.min(jnp.where(xe2 == m2, lane, big), axis=-1, keepdims=True)
    t = jnp.exp(m2 - m1)
    w1 = pg_top / (1.0 + t)
    w2 = pg_top * t / (1.0 + t)
    tm = x.shape[0]
    o1 = jnp.where(lane == e1, 1.0, 0.0)
    o2 = jnp.where(lane == e2, 1.0, 0.0)
    rr = lax.broadcasted_iota(jnp.int32, (tm, tm), 0)
    cc = lax.broadcasted_iota(jnp.int32, (tm, tm), 1)
    tri = jnp.where(cc < rr, 1.0, 0.0).astype(BF16)
    cum1 = jnp.dot(tri, o1.astype(BF16), preferred_element_type=F32)
    cum2 = jnp.dot(tri, o2.astype(BF16), preferred_element_type=F32)
    tot1 = jnp.sum(o1, axis=0, keepdims=True)
    tot2 = jnp.sum(o2, axis=0, keepdims=True)
    carry = carry_ref[0:1, :]
    rank1 = jnp.sum(o1 * (carry + cum1), axis=-1, keepdims=True)
    rank2 = jnp.sum(o2 * (carry + tot1 + cum2), axis=-1, keepdims=True)
    new = jnp.broadcast_to(carry + tot1 + tot2, carry_ref.shape)
    carry_ref[...] = new
    cnt_ref[...] = new
    cols = (e1, e2, rank1, rank2, w1, w2)
    info = jnp.zeros_like(x)
    for k, col in enumerate(cols):
        info = jnp.where(lane == k, col, info)
    info_ref[...] = info


def route(dm, logits):
    tm = dm.tm_s
    return pl.pallas_call(
        functools.partial(_route_kernel, n_exp=dm.n_exp, groups=dm.groups, epg=dm.epg),
        out_shape=(jax.ShapeDtypeStruct((dm.nt, LANE), F32), jax.ShapeDtypeStruct((SUBLANE, LANE), F32)),
        grid=(dm.nt // tm,),
        in_specs=[pl.BlockSpec((tm, LANE), lambda i: (i, 0))],
        out_specs=(pl.BlockSpec((tm, LANE), lambda i: (i, 0)), pl.BlockSpec((SUBLANE, LANE), lambda i: (0, 0))),
        scratch_shapes=[pltpu.VMEM((SUBLANE, LANE), F32)],
        compiler_params=_cparams("arbitrary"),
        name="route",
    )(logits)


def _positions_kernel(info_ref, cnt_ref, pos_ref, meta_ref, *, n_exp, tile):
    cnt = cnt_ref[0:1, :]
    padded = jnp.floor((cnt + (tile - 1.0)) * (1.0 / tile)) * tile
    a = lax.broadcasted_iota(jnp.int32, (LANE, LANE), 0)
    b = lax.broadcasted_iota(jnp.int32, (LANE, LANE), 1)
    before = jnp.where(a < b, 1.0, 0.0).astype(BF16)
    off = jnp.dot(jnp.broadcast_to(padded, (SUBLANE, LANE)).astype(BF16), before,
                  preferred_element_type=F32)[0:1, :]
    info = info_ref[...]
    lane = lax.broadcasted_iota(jnp.int32, info.shape, 1).astype(F32)
    e1, e2, r1, r2 = info[:, 0:1], info[:, 1:2], info[:, 2:3], info[:, 3:4]
    p1 = r1 + jnp.sum(jnp.where(lane == e1, off, 0.0), axis=-1, keepdims=True)
    p2 = r2 + jnp.sum(jnp.where(lane == e2, off, 0.0), axis=-1, keepdims=True)
    pos_ref[...] = jnp.where(lane == 0, p1, jnp.where(lane == 1, p2, 0.0)).astype(jnp.int32)

    end = off + padded
    end_col = jnp.sum(jnp.where(a == b, jnp.broadcast_to(end, (LANE, LANE)), 0.0), axis=1, keepdims=True)
    start = lax.broadcasted_iota(jnp.int32, (1, 2 * LANE), 1).astype(F32) * tile
    te = jnp.sum(jnp.where(end_col <= start, 1.0, 0.0), axis=0, keepdims=True)
    te = jnp.minimum(te, n_exp - 1.0)
    n_used = jnp.sum(padded, axis=-1, keepdims=True) * (1.0 / tile)
    tail = jnp.maximum(end - tile, 0.0)
    meta_ref[...] = jnp.zeros_like(meta_ref)
    meta_ref[0:1, :] = te.astype(jnp.int32)
    meta_ref[1:2, 0:LANE] = tail.astype(jnp.int32)
    meta_ref[2:3, :] = jnp.broadcast_to(n_used, (1, 2 * LANE)).astype(jnp.int32)


def positions(dm, info, cnt):
    tm = dm.tm_s
    return pl.pallas_call(
        functools.partial(_positions_kernel, n_exp=dm.n_exp, tile=dm.moe_tile),
        out_shape=(jax.ShapeDtypeStruct((dm.nt, LANE), jnp.int32),
                   jax.ShapeDtypeStruct((SUBLANE, 2 * LANE), jnp.int32)),
        grid=(dm.nt // tm,),
        in_specs=[pl.BlockSpec((tm, LANE), lambda i: (i, 0)), pl.BlockSpec((SUBLANE, LANE), lambda i: (0, 0))],
        out_specs=(pl.BlockSpec((tm, LANE), lambda i: (i, 0)),
                   pl.BlockSpec((SUBLANE, 2 * LANE), lambda i: (0, 0))),
        compiler_params=_cparams("arbitrary"),
        name="positions",
    )(info, cnt)


def _row_copy(src, src_row, dst, dst_row, sem):
    return pltpu.make_async_copy(src.at[pl.ds(src_row, 1)], dst.at[pl.ds(dst_row, 1)], sem)


def _dispatch_kernel(tail_ref, pos1_ref, pos2_ref, h_ref, xs_ref, zero_ref, sem, zsem, *, n_exp, tile, tb):
    i = pl.program_id(0)

    def zero_copy(e):
        start = pl.multiple_of(tail_ref[e], SUBLANE)
        return pltpu.make_async_copy(zero_ref, xs_ref.at[pl.ds(start, tile)], zsem)

    @pl.when(i == 0)
    def _():
        zero_ref[...] = jnp.zeros_like(zero_ref)
        for e in range(n_exp):
            zero_copy(e).start()
        for e in range(n_exp):
            zero_copy(e).wait()

    def body(t, carry):
        _row_copy(h_ref, i * tb + t, xs_ref, pos1_ref[0, 0, t], sem).start()
        _row_copy(h_ref, i * tb + t, xs_ref, pos2_ref[0, 0, t], sem).start()
        return carry

    lax.fori_loop(0, tb, body, 0, unroll=8)
    pltpu.make_async_copy(xs_ref.at[pl.ds(0, 2 * tb)], xs_ref.at[pl.ds(0, 2 * tb)], sem).wait()


def dispatch(dm, tail, pos1, pos2, h2p):
    tb, tile = dm.tm_s, dm.moe_tile
    rows = 2 * dm.nt + dm.n_exp * tile
    blk = pl.BlockSpec((1, 1, tb), lambda i, tail: (i, 0, 0), memory_space=pltpu.SMEM)
    return pl.pallas_call(
        functools.partial(_dispatch_kernel, n_exp=dm.n_exp, tile=tile, tb=tb),
        out_shape=jax.ShapeDtypeStruct((rows, dm.d // 2), U32),
        grid_spec=pltpu.PrefetchScalarGridSpec(
            num_scalar_prefetch=1, grid=(dm.nt // tb,),
            in_specs=[blk, blk, pl.BlockSpec(memory_space=pl.ANY)],
            out_specs=pl.BlockSpec(memory_space=pl.ANY),
            scratch_shapes=[pltpu.VMEM((tile, dm.d // 2), U32), pltpu.SemaphoreType.DMA,
                            pltpu.SemaphoreType.DMA]),
        compiler_params=_cparams("arbitrary"),
        name="dispatch",
    )(tail, pos1, pos2, h2p)


def _experts_kernel(te_ref, nu_ref, x_ref, wg_ref, wu_ref, wd_ref, y_ref):
    @pl.when(pl.program_id(0) < nu_ref[0])
    def _():
        x = _unpack_bf16_pairs(x_ref[...]).astype(BF16)
        hg = jnp.dot(x, wg_ref[0], preferred_element_type=F32)
        hu = jnp.dot(x, wu_ref[0], preferred_element_type=F32)
        a = (_silu(hg) * hu).astype(BF16)
        y_ref[...] = _pack_bf16_pairs(jnp.dot(a, wd_ref[0], preferred_element_type=F32))


def experts(dm, te, n_used, xs, wg, wu, wd):
    tile, d, fe = dm.moe_tile, dm.d, dm.fe
    rows = xs.shape[0]
    last = lambda r, nu: jnp.minimum(r, nu[0] - 1)
    return pl.pallas_call(
        _experts_kernel,
        out_shape=jax.ShapeDtypeStruct((rows, d // 2), U32),
        grid_spec=pltpu.PrefetchScalarGridSpec(
            num_scalar_prefetch=2, grid=(rows // tile,),
            in_specs=[pl.BlockSpec((tile, d // 2), lambda r, te, nu: (last(r, nu), 0)),
                      pl.BlockSpec((1, d, fe), lambda r, te, nu: (te[last(r, nu)], 0, 0)),
                      pl.BlockSpec((1, d, fe), lambda r, te, nu: (te[last(r, nu)], 0, 0)),
                      pl.BlockSpec((1, fe, d), lambda r, te, nu: (te[last(r, nu)], 0, 0))],
            out_specs=pl.BlockSpec((tile, d // 2), lambda r, te, nu: (last(r, nu), 0))),
        compiler_params=_cparams("arbitrary"),
        name="experts",
    )(te, n_used, xs, wg, wu, wd)


def _combine_kernel(pos1_ref, pos2_ref, ys_ref, info_ref, x1_ref, mod_ref, o_ref, b1_ref, b2_ref, sem,
                    *, d, tb):
    def body(t, carry):
        _row_copy(ys_ref, pos1_ref[0, 0, t], b1_ref, t, sem).start()
        _row_copy(ys_ref, pos2_ref[0, 0, t], b2_ref, t, sem).start()
        return carry

    lax.fori_loop(0, tb, body, 0, unroll=8)
    pltpu.make_async_copy(ys_ref.at[pl.ds(0, tb)], b1_ref, sem).wait()
    pltpu.make_async_copy(ys_ref.at[pl.ds(0, tb)], b2_ref, sem).wait()
    info = info_ref[...]
    y = info[:, 4:5] * _unpack_bf16_pairs(b1_ref[...]) + info[:, 5:6] * _unpack_bf16_pairs(b2_ref[...])
    o_ref[...] = x1_ref[...] + mod_ref[0, :, 5 * d:6 * d] * y


def combine(dm, pos1, pos2, ys, info, x1, mod_l):
    tb, d = dm.tm_s, dm.d
    cid = _cond_id(dm, tb)
    blk = pl.BlockSpec((1, 1, tb), lambda i: (i, 0, 0), memory_space=pltpu.SMEM)
    row = lambda i: (i, 0)
    return pl.pallas_call(
        functools.partial(_combine_kernel, d=d, tb=tb),
        out_shape=jax.ShapeDtypeStruct((dm.nt, d), F32),
        grid=(dm.nt // tb,),
        in_specs=[blk, blk, pl.BlockSpec(memory_space=pl.ANY), pl.BlockSpec((tb, LANE), row),
                  pl.BlockSpec((tb, d), row), pl.BlockSpec((1, 1, 6 * d), lambda i: (cid(i), 0, 0))],
        out_specs=pl.BlockSpec((tb, d), row),
        scratch_shapes=[pltpu.VMEM((tb, d // 2), U32), pltpu.VMEM((tb, d // 2), U32),
                        pltpu.SemaphoreType.DMA],
        compiler_params=_cparams("arbitrary"),
        name="combine",
    )(pos1, pos2, ys, info, x1, mod_l)


def _dft_tables(length):
    k = lax.broadcasted_iota(jnp.int32, (length, length), 0)
    s = lax.broadcasted_iota(jnp.int32, (length, length), 1)
    ang = ((k * s) % (2 * length)).astype(F32) * (math.pi / length)
    return jnp.cos(ang).astype(BF16), jnp.sin(ang).astype(BF16)


def _rope_tables(dm, tile):
    length = dm.dec_seq
    n_freq = dm.rope // 4
    half = dm.rope // 2
    pos = jnp.arange(length, dtype=jnp.int32)
    row = (pos // dm.grid_w).astype(F32)
    col = (pos % dm.grid_w).astype(F32)
    inv = jnp.power(ROPE_BASE, -jnp.arange(n_freq, dtype=F32) / n_freq)
    ang = jnp.concatenate([row[:, None] * inv[None, :], col[:, None] * inv[None, :]], axis=-1)
    cos, sin = jnp.cos(ang), jnp.sin(ang)
    ones = lambda n: jnp.ones((length, n), F32)
    zeros = lambda n: jnp.zeros((length, n), F32)
    tail = LANE - dm.nope - dm.rope
    cos_t = jnp.concatenate([ones(dm.nope), cos, cos, ones(tail)], axis=1)
    sa_t = jnp.concatenate([zeros(dm.nope + half), sin, zeros(tail)], axis=1)
    sb_t = jnp.concatenate([zeros(dm.nope), -sin, zeros(half + tail)], axis=1)
    ident = lambda v: jnp.full((tile, LANE), v, F32)
    return (jnp.concatenate([ident(1.0), cos_t]), jnp.concatenate([ident(0.0), sa_t]),
            jnp.concatenate([ident(0.0), sb_t]))


def _pack_w_in(dm, w_in):
    o = 0
    parts = {}
    for name, n in (("hy", 3 * dm.hw), ("cq", dm.q_lora), ("ckv", dm.kv_lora), ("kpe", dm.rope),
                    ("cf", 2 * dm.cw), ("gate", 3 * dm.d)):
        parts[name] = w_in[:, o:o + n]
        o += n
    z = lambda n: jnp.zeros((w_in.shape[0], n), w_in.dtype)
    small = [parts["cq"], parts["ckv"], z(dm.nope), parts["kpe"], z(LANE - dm.nope - dm.rope)]
    used = dm.q_lora + dm.kv_lora + LANE
    small.append(z(dm.tn - used))
    return jnp.concatenate(small + [parts["gate"], parts["hy"], parts["cf"]], axis=1).astype(BF16)


def _forward(dm, x_prompt, x_sample, cache_ckv, cache_kpe, c, c_ctx, w_mod, b_mod, norm_mix, norm_ffn,
             w_in, hy_short, hy_f_w1, hy_f_b1, hy_f_w2, hy_f_b2, hy_f_w3, hy_decay, hy_bias, w_hy_out,
             q_a_norm, w_uq, kv_a_norm, w_ukv, q_norm, k_norm, w_mla_out,
             cf_dw, cf_dw_b, cf_ln_g, cf_ln_b, w_cf_out, b_cf_out, w_out,
             w_rg, b_rg, w_re, b_re, w_eg, w_eu, w_ed):
    d, hw, cw, heads = dm.d, dm.hw, dm.cw, dm.heads
    n_ctx, nt = dm.n_ctx, dm.nt
    gate_col0, hy_col0, cf_col0 = 0, 3 * d, 3 * d + 3 * hw
    ckv_blk = dm.q_lora // dm.kv_lora
    kpe_blk = (dm.q_lora + dm.kv_lora) // LANE
    kpe_lane0 = dm.q_lora + dm.kv_lora + dm.nope

    x = jnp.concatenate([x_prompt.reshape(n_ctx, d), x_sample.reshape(dm.n_lat, d)], axis=0)
    cond8 = jnp.concatenate([c_ctx[None, :], c, jnp.zeros((8 - 1 - dm.dec_batch, d), F32)], axis=0)
    mod = adaln(dm, cond8, w_mod, b_mod)

    dft = {dm.seq: _dft_tables(dm.seq), dm.dec_seq: _dft_tables(dm.dec_seq)}
    cos_t, sa_t, sb_t = _rope_tables(dm, dm.tm_s)
    scale = float(dm.dk) ** -0.5
    tail = LANE - dm.dk
    tc_ctx = min(hw, 1024)
    tq_ctx = dm.seq
    pairs_ctx = heads // 2
    tm_cache = min(dm.dec_batch * dm.past, dm.tm_s)

    ckv_out, kpe_out = [], []
    for l in range(dm.depth):
        mod_l = mod[l][:, None, :]
        w_packed = _pack_w_in(dm, w_in[l])
        p2, p = inproj(dm, x, mod_l, norm_mix[l][None, :], w_packed)

        w1p = jnp.zeros((LANE, dm.hy_fhid), F32).at[:1 + 2 * dm.hy_freqs].set(hy_f_w1[l])
        z2 = None
        for (length, row0, nseq, tc) in ((dm.seq, 0, dm.batch, tc_ctx),
                                         (dm.dec_seq, n_ctx, dm.dec_batch, min(hw, dm.tc_hy))):
            cmat, smat = dft[length]
            filt = hyena_filters(dm, length, w1p, hy_f_b1[l][None, :], hy_f_w2[l], hy_f_b2[l][None, :],
                                 hy_f_w3[l], hy_decay[l][None, :])
            kc, ks, kn = hyena_spectra(dm, length, filt, cmat, smat)
            z2 = hyena_conv(dm, p, hy_col0, row0, nseq, length, tc, hy_short[l], hy_bias[l], kc, ks, kn,
                            cmat, smat, prev=z2)

        w_uq_p = jnp.pad(w_uq[l].reshape(dm.q_lora, heads, dm.dk), ((0, 0), (0, 0), (0, tail)))
        w_uq_p = w_uq_p.reshape(dm.q_lora, heads * LANE).astype(BF16)
        wkv = w_ukv[l].reshape(dm.kv_lora, heads, dm.nope + dm.vh)
        wk_p = jnp.pad(wkv[:, :, :dm.nope], ((0, 0), (0, 0), (0, LANE - dm.nope)))
        w_ukv_p = jnp.concatenate([wk_p.reshape(dm.kv_lora, heads * LANE),
                                   wkv[:, :, dm.nope:].reshape(dm.kv_lora, heads * dm.vh)], axis=1).astype(BF16)
        gq = jnp.concatenate([q_norm[l] * scale, jnp.zeros((tail,), F32)])[None, :]
        gk = jnp.concatenate([k_norm[l], jnp.zeros((tail,), F32)])[None, :]
        q = mla_q(dm, p2, q_a_norm[l][None, :], w_uq_p, gq, cos_t, sa_t, sb_t)
        k, v, ckv_n = mla_kv(dm, p2, ckv_blk, p2, kpe_blk, nt, dm.tm_s, kv_a_norm[l][None, :], w_ukv_p, gk,
                             cos_t, sa_t, sb_t, _rope_blk(dm, dm.tm_s), True)
        kpe_c = jnp.pad(cache_kpe[:, l].reshape(dm.dec_batch * dm.past, dm.rope),
                        ((0, 0), (dm.nope, LANE - dm.dk)))
        k_c, v_c, _ = mla_kv(dm, cache_ckv[:, l].reshape(dm.dec_batch * dm.past, dm.kv_lora), 0, kpe_c, 0,
                             dm.dec_batch * dm.past, tm_cache, kv_a_norm[l][None, :], w_ukv_p, gk,
                             cos_t, sa_t, sb_t, lambda i: 0, False)
        o = attention(dm, q, [(k, v, dm.seq, 0)], 0, dm.batch, dm.seq, tq_ctx, pairs_ctx)
        o = attention(dm, q, [(k, v, dm.dec_seq, n_ctx), (k_c, v_c, dm.past, 0)], n_ctx, dm.dec_batch,
                      dm.dec_seq, min(dm.tq, dm.dec_seq), 1, prev=o)
        ckv_out.append(ckv_n[:n_ctx].reshape(dm.batch, dm.seq, dm.kv_lora))
        kpe_out.append(p2[:n_ctx, kpe_lane0:kpe_lane0 + dm.rope].reshape(dm.batch, dm.seq, dm.rope))

        hc = conformer(dm, p, cf_col0, 0, dm.batch, dm.seq, cf_dw[l], cf_dw_b[l][None, :],
                       cf_ln_g[l][None, :], cf_ln_b[l][None, :])
        hc = conformer(dm, p, cf_col0, n_ctx, dm.dec_batch, dm.dec_seq, cf_dw[l], cf_dw_b[l][None, :],
                       cf_ln_g[l][None, :], cf_ln_b[l][None, :], prev=hc)

        merged = merge(dm, z2, o, hc, p, gate_col0, w_hy_out[l].astype(BF16), w_mla_out[l].astype(BF16),
                       w_cf_out[l].astype(BF16), b_cf_out[l][None, :])
        wr = jnp.concatenate([w_re[l], w_rg[l], jnp.zeros((d, LANE - dm.n_exp - dm.groups), F32)],
                             axis=1).astype(BF16)
        br = jnp.concatenate([b_re[l], b_rg[l], jnp.zeros((LANE - dm.n_exp - dm.groups,), F32)])[None, :]
        x1, h2p, logits = outproj(dm, merged, w_out[l].astype(BF16), x, mod_l, norm_ffn[l][None, :], wr, br)

        info, cnt = route(dm, logits)
        pos, meta = positions(dm, info, cnt)
        pos1 = pos[:, 0].reshape(nt // dm.tm_s, 1, dm.tm_s)
        pos2 = pos[:, 1].reshape(nt // dm.tm_s, 1, dm.tm_s)
        n_tiles = (2 * nt + dm.n_exp * dm.moe_tile) // dm.moe_tile
        xs = dispatch(dm, meta[1, :dm.n_exp], pos1, pos2, h2p)
        ys = experts(dm, meta[0, :n_tiles], meta[2, :1], xs, w_eg[l].astype(BF16), w_eu[l].astype(BF16),
                     w_ed[l].astype(BF16))
        x = combine(dm, pos1, pos2, ys, info, x1, mod_l)

    y_prompt = x[:n_ctx].reshape(dm.batch, dm.seq, d)
    y_sample = x[n_ctx:].reshape(dm.dec_batch, dm.dec_seq, d)
    return y_prompt, y_sample, jnp.stack(ckv_out, axis=1), jnp.stack(kpe_out, axis=1)


def kernel(x_prompt, x_sample, cache_ckv, cache_kpe, c, c_ctx, w_mod, b_mod, norm_mix, norm_ffn, w_in, hy_short, hy_f_w1, hy_f_b1, hy_f_w2, hy_f_b2, hy_f_w3, hy_decay, hy_bias, w_hy_out, q_a_norm, w_uq, kv_a_norm, w_ukv, q_norm, k_norm, w_mla_out, cf_dw, cf_dw_b, cf_ln_g, cf_ln_b, w_cf_out, b_cf_out, w_out, w_rg, b_rg, w_re, b_re, w_eg, w_eu, w_ed):
    return _forward(Dims(), x_prompt, x_sample, cache_ckv, cache_kpe, c, c_ctx, w_mod, b_mod, norm_mix,
                    norm_ffn, w_in, hy_short, hy_f_w1, hy_f_b1, hy_f_w2, hy_f_b2, hy_f_w3, hy_decay, hy_bias,
                    w_hy_out, q_a_norm, w_uq, kv_a_norm, w_ukv, q_norm, k_norm, w_mla_out,
                    cf_dw, cf_dw_b, cf_ln_g, cf_ln_b, w_cf_out, b_cf_out, w_out,
                    w_rg, b_rg, w_re, b_re, w_eg, w_eu, w_ed)
```

```python
import functools
import math
from typing import NamedTuple

import jax
import jax.numpy as jnp
from jax import lax
from jax.experimental import pallas as pl
from jax.experimental.pallas import tpu as pltpu

F32 = jnp.float32
BF16 = jnp.bfloat16
U32 = jnp.uint32
EPS = 1e-6
LANE = 128
SUBLANE = 8
CONV_SUB = 256
KEY_CHUNK = 512
VMEM_LIMIT_BYTES = 56 << 20
ROPE_BASE = 10000.0
HY_ORDER = 2
HY_SHORT = 3
HALO = 16
CONV_CHUNK = 16


class Dims(NamedTuple):
    d: int = 2048
    batch: int = 32
    seq: int = 256
    depth: int = 4
    dec_batch: int = 4
    dec_seq: int = 2048
    past: int = 256
    grid_w: int = 64
    hw: int = 1024
    hy_freqs: int = 16
    hy_fhid: int = 64
    heads: int = 16
    nope: int = 64
    rope: int = 32
    vh: int = 64
    q_lora: int = 512
    kv_lora: int = 256
    cw: int = 1024
    cf_k: int = 31
    groups: int = 4
    epg: int = 4
    fe: int = 512
    tm: int = 1024
    tm_s: int = 512
    tn: int = 1024
    tq: int = 512
    tc_hy: int = 256
    hy_nb: int = 2
    hy_rb: int = 1024
    moe_tile: int = 256

    @property
    def n_ctx(self):
        return self.batch * self.seq

    @property
    def n_lat(self):
        return self.dec_batch * self.dec_seq

    @property
    def nt(self):
        return self.n_ctx + self.n_lat

    @property
    def n_exp(self):
        return self.groups * self.epg

    @property
    def dk(self):
        return self.nope + self.rope


def _cparams(*sem):
    return pltpu.CompilerParams(dimension_semantics=sem, vmem_limit_bytes=VMEM_LIMIT_BYTES)


def _resident(shape, index_map):
    return pl.BlockSpec(shape, index_map, pipeline_mode=pl.Buffered(1))


def _silu(x):
    return x * jax.nn.sigmoid(x)


def _pack_bf16_pairs(x):
    n = x.shape[1] // 2
    lo = lax.bitcast_convert_type(x[:, :n].astype(jnp.bfloat16).astype(F32), U32) >> 16
    hi = lax.bitcast_convert_type(x[:, n:].astype(jnp.bfloat16).astype(F32), U32) & jnp.uint32(0xFFFF0000)
    return hi | lo


def _unpack_bf16_pairs(u):
    lo = lax.bitcast_convert_type(u << 16, F32)
    hi = lax.bitcast_convert_type(u & jnp.uint32(0xFFFF0000), F32)
    return jnp.concatenate([lo, hi], axis=1)


def _rms(x, g):
    return x * lax.rsqrt(jnp.mean(x * x, axis=-1, keepdims=True) + EPS) * g


def _cond_id(dm, tile):
    n_ctx_tiles = dm.n_ctx // tile
    per_seq = dm.dec_seq // tile
    return lambda i: jnp.where(i < n_ctx_tiles, 0, 1 + (i - n_ctx_tiles) // per_seq)


def _rope_blk(dm, tile):
    n_ctx_tiles = dm.n_ctx // tile
    per_seq = dm.dec_seq // tile
    return lambda i: jnp.where(i < n_ctx_tiles, 0, (i - n_ctx_tiles) % per_seq)


def _adaln_kernel(c_ref, w_ref, b_ref, o_ref):
    a = _silu(c_ref[...]).astype(BF16)
    o_ref[0] = jnp.dot(a, w_ref[0].astype(BF16), preferred_element_type=F32) + b_ref[0]


def adaln(dm, cond8, w_mod, b_mod):
    depth, d, n = w_mod.shape
    tn = min(n, 1024)
    return pl.pallas_call(
        _adaln_kernel,
        out_shape=jax.ShapeDtypeStruct((depth, 8, n), F32),
        grid=(depth, n // tn),
        in_specs=[pl.BlockSpec((8, d), lambda l, j: (0, 0)),
                  pl.BlockSpec((1, d, tn), lambda l, j: (l, 0, j)),
                  pl.BlockSpec((1, 1, tn), lambda l, j: (l, 0, j))],
        out_specs=pl.BlockSpec((1, 8, tn), lambda l, j: (l, 0, j)),
        compiler_params=_cparams("arbitrary", "arbitrary"),
        name="adaln",
    )(cond8, w_mod, b_mod.reshape(depth, 1, n))


def _inproj_kernel(x_ref, mod_ref, g_ref, w_ref, p2_ref, p_ref, hn_ref, *, d):
    j = pl.program_id(1)

    @pl.when(j == 0)
    def _():
        y = _rms(x_ref[...], g_ref[...])
        sh = mod_ref[0, :, 0:d]
        sc = mod_ref[0, :, d:2 * d]
        hn_ref[...] = (y * (1.0 + sc) + sh).astype(BF16)

    acc = jnp.dot(hn_ref[...], w_ref[...], preferred_element_type=F32)
    p_ref[...] = acc.astype(BF16)

    @pl.when(j == 0)
    def _():
        p2_ref[...] = acc


def inproj(dm, x, mod_l, g, w_packed):
    nt, d = x.shape
    tm, tn = dm.tm, dm.tn
    ncols = w_packed.shape[1]
    nj = ncols // tn
    cid = _cond_id(dm, tm)
    return pl.pallas_call(
        functools.partial(_inproj_kernel, d=d),
        out_shape=(jax.ShapeDtypeStruct((nt, tn), F32),
                   jax.ShapeDtypeStruct((nt, ncols - tn), BF16)),
        grid=(nt // tm, nj),
        in_specs=[pl.BlockSpec((tm, d), lambda i, j: (i, 0)),
                  pl.BlockSpec((1, 1, 6 * d), lambda i, j: (cid(i), 0, 0)),
                  pl.BlockSpec((1, d), lambda i, j: (0, 0)),
                  pl.BlockSpec((d, tn), lambda i, j: (0, j))],
        out_specs=(pl.BlockSpec((tm, tn), lambda i, j: (i, 0)),
                   pl.BlockSpec((tm, tn), lambda i, j: (i, jnp.maximum(j - 1, 0)))),
        scratch_shapes=[pltpu.VMEM((tm, d), BF16)],
        compiler_params=_cparams("arbitrary", "arbitrary"),
        name="inproj",
    )(x, mod_l, g, w_packed)


def _dot3(a, b):
    ah = a.astype(BF16)
    al = (a - ah.astype(F32)).astype(BF16)
    bh = b.astype(BF16)
    bl = (b - bh.astype(F32)).astype(BF16)
    return (jnp.dot(ah, bh, preferred_element_type=F32) + jnp.dot(al, bh, preferred_element_type=F32)
            + jnp.dot(ah, bl, preferred_element_type=F32))


def _hyfilt_kernel(w1_ref, b1_ref, w2_ref, b2_ref, w3_ref, dec_ref, f_ref, *, length, nfreq):
    row = lax.broadcasted_iota(jnp.int32, (length, LANE), 0).astype(F32)
    lane = lax.broadcasted_iota(jnp.int32, (length, LANE), 1)
    tn = row / float(length)
    fr = jnp.where(lane <= nfreq, lane, lane - nfreq).astype(F32)
    ang = (2.0 * math.pi) * tn * fr
    feat = jnp.where(lane == 0, tn, jnp.where(lane <= nfreq, jnp.cos(ang),
                                              jnp.where(lane <= 2 * nfreq, jnp.sin(ang), 0.0)))
    h = jnp.sin(_dot3(feat, w1_ref[...]) + b1_ref[...])
    h = jnp.sin(_dot3(h, w2_ref[...]) + b2_ref[...])
    f_ref[...] = _dot3(h, w3_ref[...]) * jnp.exp(-tn[:, 0:1] * dec_ref[...])


def hyena_filters(dm, length, w1p, b1, w2, b2, w3, decay):
    ncol = w3.shape[1]
    tn = min(ncol, 1024)
    fh = w2.shape[0]
    return pl.pallas_call(
        functools.partial(_hyfilt_kernel, length=length, nfreq=dm.hy_freqs),
        out_shape=jax.ShapeDtypeStruct((length, ncol), F32),
        grid=(ncol // tn,),
        in_specs=[pl.BlockSpec((LANE, fh), lambda j: (0, 0)),
                  pl.BlockSpec((1, fh), lambda j: (0, 0)),
                  pl.BlockSpec((fh, fh), lambda j: (0, 0)),
                  pl.BlockSpec((1, fh), lambda j: (0, 0)),
                  pl.BlockSpec((fh, tn), lambda j: (0, j)),
                  pl.BlockSpec((1, tn), lambda j: (0, j))],
        out_specs=pl.BlockSpec((length, tn), lambda j: (0, j)),
        compiler_params=_cparams("arbitrary"),
        name="hyena_filters",
    )(w1p, b1, w2, b2, w3, decay)


def _hyspec_kernel(hf_ref, hb_ref, c_ref, s_ref, kc_ref, ks_ref, kn_ref, *, length):
    row = lax.broadcasted_iota(jnp.int32, (length, 1), 0)
    hf = hf_ref[...]
    hb0 = jnp.where(row == 0, 0.0, hb_ref[...])
    sp = hf + hb0
    sm = hf - hb0
    inv = 1.0 / length
    kc = jnp.dot(c_ref[...], sp.astype(BF16), preferred_element_type=F32)
    kc_ref[0] = kc * jnp.where(row == 0, 0.5 * inv, inv)
    ks_ref[0] = jnp.dot(s_ref[...], sm.astype(BF16), preferred_element_type=F32) * inv
    sgn = (1 - 2 * (row & 1)).astype(F32)
    kn = jnp.sum(sgn * sp, axis=0, keepdims=True) * (0.5 * inv)
    kn_ref[0] = jnp.broadcast_to(kn, kn_ref.shape[1:])


def hyena_spectra(dm, length, filt, cmat, smat):
    hw = dm.hw
    tc = min(hw, 512)
    nct = hw // tc
    shp = jax.ShapeDtypeStruct((HY_ORDER, length, hw), F32)
    return pl.pallas_call(
        functools.partial(_hyspec_kernel, length=length),
        out_shape=(shp, shp, jax.ShapeDtypeStruct((HY_ORDER, 8, hw), F32)),
        grid=(HY_ORDER, nct),
        in_specs=[pl.BlockSpec((length, tc), lambda n, j: (0, (2 * n) * nct + j)),
                  pl.BlockSpec((length, tc), lambda n, j: (0, (2 * n + 1) * nct + j)),
                  _resident((length, length), lambda n, j: (0, 0)),
                  _resident((length, length), lambda n, j: (0, 0))],
        out_specs=(pl.BlockSpec((1, length, tc), lambda n, j: (n, 0, j)),
                   pl.BlockSpec((1, length, tc), lambda n, j: (n, 0, j)),
                   pl.BlockSpec((1, 8, tc), lambda n, j: (n, 0, j))),
        compiler_params=_cparams("arbitrary", "arbitrary"),
        name="hyena_spectra",
    )(filt, filt, cmat, smat)


def _hyconv_kernel(v_ref, x1_ref, x2_ref, swv_ref, sw1_ref, sw2_ref, bias_ref, kc_ref, ks_ref, kn_ref,
                   c_ref, s_ref, *rest, length, nb, rb):
    o_ref, zb_ref, zc_ref, zs_ref, z1_ref = rest[-5:]
    tc = v_ref.shape[1]
    nblk = length // rb
    rowi = lax.broadcasted_iota(jnp.int32, (rb, 1), 0)
    sgn = (1 - 2 * (rowi & 1)).astype(F32)
    halo = 2 * SUBLANE

    def lanes(x):
        return jnp.concatenate([x] * nb, axis=1) if nb > 1 else x

    def short(x_ref, w_ref, q, r):
        base = q * length + r * rb
        x = x_ref[base:base + rb, :].astype(F32)
        prev = (jnp.zeros((1, tc), F32) if r == 0
                else x_ref[base - halo:base, :].astype(F32)[halo - 1:halo, :])
        nxt = (jnp.zeros((1, tc), F32) if r == nblk - 1
               else x_ref[base + rb:base + rb + halo, :].astype(F32)[0:1, :])
        xp = jnp.where(rowi == 0, prev, pltpu.roll(x, 1, 0))
        xn = jnp.where(rowi == rb - 1, nxt, pltpu.roll(x, rb - 1, 0))
        return w_ref[0:1, :] * xp + w_ref[1:2, :] * x + w_ref[2:3, :] * xn

    def forward(n):
        for r in range(nblk):
            rows = slice(r * rb, (r + 1) * rb)
            uc = jnp.dot(c_ref[rows, :], zb_ref[...], preferred_element_type=F32)
            us = jnp.dot(s_ref[rows, :], zb_ref[...], preferred_element_type=F32)
            kc = lanes(kc_ref[n, rows, :])
            ks = lanes(ks_ref[n, rows, :])
            zc_ref[rows, :] = (uc * kc - us * ks).astype(BF16)
            zs_ref[rows, :] = (uc * ks + us * kc).astype(BF16)

    def inverse(n, r, un):
        rows = slice(r * rb, (r + 1) * rb)
        y = jnp.dot(c_ref[rows, :], zc_ref[...], preferred_element_type=F32)
        y = y + jnp.dot(s_ref[rows, :], zs_ref[...], preferred_element_type=F32)
        return y + sgn * (un * lanes(kn_ref[n, 0:1, :]))

    un = []
    for q in range(nb):
        acc = jnp.zeros((1, tc), F32)
        for r in range(nblk):
            zv = short(v_ref, swv_ref, q, r)
            zb_ref[r * rb:(r + 1) * rb, q * tc:(q + 1) * tc] = zv.astype(BF16)
            acc = acc + jnp.sum(sgn * zv, axis=0, keepdims=True)
        un.append(acc)
    un = jnp.concatenate(un, axis=1) if nb > 1 else un[0]
    forward(0)
    un2 = [jnp.zeros((1, tc), F32) for _ in range(nb)]
    for r in range(nblk):
        rows = slice(r * rb, (r + 1) * rb)
        y = inverse(0, r, un)
        for q in range(nb):
            cols = slice(q * tc, (q + 1) * tc)
            zv = short(v_ref, swv_ref, q, r)
            z1 = short(x1_ref, sw1_ref, q, r) * (y[:, cols] + zv * bias_ref[0:1, :])
            z1_ref[rows, cols] = z1
            zb_ref[rows, cols] = z1.astype(BF16)
            un2[q] = un2[q] + jnp.sum(sgn * z1, axis=0, keepdims=True)
    un2 = jnp.concatenate(un2, axis=1) if nb > 1 else un2[0]
    forward(1)
    for r in range(nblk):
        rows = slice(r * rb, (r + 1) * rb)
        y = inverse(1, r, un2)
        for q in range(nb):
            cols = slice(q * tc, (q + 1) * tc)
            z2 = short(x2_ref, sw2_ref, q, r) * (y[:, cols] + z1_ref[rows, cols] * bias_ref[1:2, :])
            o_ref[q * length + r * rb:q * length + (r + 1) * rb, :] = z2.astype(BF16)


def hyena_conv(dm, p, hy_col0, row0, nseq, length, tc, nb, short_w, bias, kc, ks, kn, cmat, smat, prev=None):
    hw = dm.hw
    nct = hw // tc
    rows = nb * length
    rb0 = row0 // rows
    cb0 = hy_col0 // tc
    rb = min(length, dm.hy_rb)
    spec = _resident if rows * tc >= (1 << 20) else pl.BlockSpec

    def part(k):
        return spec((rows, tc), lambda j, s, k=k: (rb0 + s, cb0 + k * nct + j))

    def sw(k):
        return pl.BlockSpec((HY_SHORT, tc), lambda j, s, k=k: (0, k * nct + j))

    in_specs = [part(0), part(1), part(2), sw(0), sw(1), sw(2),
                pl.BlockSpec((HY_ORDER, tc), lambda j, s: (0, j)),
                _resident((HY_ORDER, length, tc), lambda j, s: (0, 0, j)),
                _resident((HY_ORDER, length, tc), lambda j, s: (0, 0, j)),
                pl.BlockSpec((HY_ORDER, 8, tc), lambda j, s: (0, 0, j)),
                _resident((length, length), lambda j, s: (0, 0)),
                _resident((length, length), lambda j, s: (0, 0))]
    args = [p, p, p, short_w, short_w, short_w, bias, kc, ks, kn, cmat, smat]
    aliases = {}
    if prev is not None:
        in_specs.append(pl.BlockSpec(memory_space=pl.ANY))
        args.append(prev)
        aliases = {len(args) - 1: 0}
    return pl.pallas_call(
        functools.partial(_hyconv_kernel, length=length, nb=nb, rb=rb),
        out_shape=jax.ShapeDtypeStruct((dm.nt, hw), BF16),
        grid=(nct, nseq // nb),
        in_specs=in_specs,
        out_specs=pl.BlockSpec((rows, tc), lambda j, s: (rb0 + s, j)),
        scratch_shapes=[pltpu.VMEM((length, nb * tc), BF16), pltpu.VMEM((length, nb * tc), BF16),
                        pltpu.VMEM((length, nb * tc), BF16), pltpu.VMEM((length, nb * tc), F32)],
        input_output_aliases=aliases,
        compiler_params=_cparams("arbitrary", "arbitrary"),
        name="hyena_conv_%d" % length,
    )(*args)


def _store_heads(x, gain, o_ref, rot, *, heads, dk):
    ones = jnp.ones((LANE, LANE), BF16)
    for h in range(heads):
        xh = x[:, h * LANE:(h + 1) * LANE]
        ss = jnp.dot((xh * xh).astype(BF16), ones, preferred_element_type=F32)
        xh = xh * lax.rsqrt(ss * (1.0 / dk) + EPS) * gain
        if rot is not None:
            xh = xh * rot[0] + pltpu.roll(xh, LANE // 2, 1) * rot[1]
        o_ref[:, h * LANE:(h + 1) * LANE] = xh.astype(BF16)


def _store_heads_by_segment(x, gain, o_ref, cos_ref, sin_ref, n_ctx_tiles, **kw):
    is_lat = pl.program_id(0) >= n_ctx_tiles

    @pl.when(is_lat)
    def _():
        _store_heads(x, gain, o_ref, (cos_ref[...], sin_ref[...]), **kw)

    @pl.when(jnp.logical_not(is_lat))
    def _():
        _store_heads(x, gain, o_ref, None, **kw)


def _q_kernel(cq_ref, ga_ref, w_ref, gh_ref, cos_ref, sin_ref, o_ref, *, heads, dk, n_ctx_tiles):
    y = _rms(cq_ref[...], ga_ref[...])
    q = jnp.dot(y.astype(BF16), w_ref[...], preferred_element_type=F32)
    _store_heads_by_segment(q, gh_ref[...], o_ref, cos_ref, sin_ref, n_ctx_tiles, heads=heads, dk=dk)


def mla_q(dm, p2, ga, w_uq, gh, cos_t, sin_t):
    tm = dm.tm_s
    rb = _rope_blk(dm, tm)
    tab = pl.BlockSpec((tm, LANE), lambda i: (rb(i), 0))
    return pl.pallas_call(
        functools.partial(_q_kernel, heads=dm.heads, dk=dm.dk, n_ctx_tiles=dm.n_ctx // tm),
        out_shape=jax.ShapeDtypeStruct((dm.nt, dm.heads * LANE), BF16),
        grid=(dm.nt // tm,),
        in_specs=[pl.BlockSpec((tm, dm.q_lora), lambda i: (i, 0)),
                  pl.BlockSpec((1, dm.q_lora), lambda i: (0, 0)),
                  pl.BlockSpec((dm.q_lora, dm.heads * LANE), lambda i: (0, 0)),
                  pl.BlockSpec((1, LANE), lambda i: (0, 0)),
                  tab, tab],
        out_specs=pl.BlockSpec((tm, dm.heads * LANE), lambda i: (i, 0)),
        compiler_params=_cparams("arbitrary"),
        name="mla_q",
    )(p2, ga, w_uq, gh, cos_t, sin_t)


def _kv_kernel(ckv_ref, kpe_ref, ga_ref, w_ref, gk_ref, *rest, heads, dk, n_ctx_tiles, from_cache):
    k_ref, v_ref, cn_ref = rest[-3:]
    x = ckv_ref[...]
    if not from_cache:
        x = _rms(x, ga_ref[...])
    cn_ref[...] = x
    kv = jnp.dot(x.astype(BF16), w_ref[...], preferred_element_type=F32)
    kpe = kpe_ref[...]
    k = kv[:, :heads * LANE] + jnp.concatenate([kpe] * heads, axis=1)
    if from_cache:
        _store_heads(k, gk_ref[...], k_ref, None, heads=heads, dk=dk)
    else:
        _store_heads_by_segment(k, gk_ref[...], k_ref, rest[0], rest[1], n_ctx_tiles, heads=heads, dk=dk)
    v_ref[...] = kv[:, heads * LANE:].astype(BF16)


def mla_kv(dm, ckv_src, ckv_blk, kpe_src, kpe_blk, nrows, tm, ga, w_ukv, gk, rope=None):
    hk = dm.heads * LANE
    hv = dm.heads * dm.vh
    in_specs = [pl.BlockSpec((tm, dm.kv_lora), lambda i: (i, ckv_blk)),
                pl.BlockSpec((tm, LANE), lambda i: (i, kpe_blk)),
                pl.BlockSpec((1, dm.kv_lora), lambda i: (0, 0)),
                pl.BlockSpec((dm.kv_lora, hk + hv), lambda i: (0, 0)),
                pl.BlockSpec((1, LANE), lambda i: (0, 0))]
    args = [ckv_src, kpe_src, ga, w_ukv, gk]
    if rope is not None:
        rb = _rope_blk(dm, tm)
        in_specs += [pl.BlockSpec((tm, LANE), lambda i: (rb(i), 0))] * 2
        args += list(rope)
    return pl.pallas_call(
        functools.partial(_kv_kernel, heads=dm.heads, dk=dm.dk, n_ctx_tiles=dm.n_ctx // tm,
                          from_cache=rope is None),
        out_shape=(jax.ShapeDtypeStruct((nrows, hk), BF16),
                   jax.ShapeDtypeStruct((nrows, hv), BF16),
                   jax.ShapeDtypeStruct((nrows, dm.kv_lora), F32)),
        grid=(nrows // tm,),
        in_specs=in_specs,
        out_specs=(pl.BlockSpec((tm, hk), lambda i: (i, 0)),
                   pl.BlockSpec((tm, hv), lambda i: (i, 0)),
                   pl.BlockSpec((tm, dm.kv_lora), lambda i: (i, 0))),
        compiler_params=_cparams("arbitrary"),
        name="mla_kv_trunk" if rope is not None else "mla_kv_cache",
    )(*args)


def _attn_kernel(q_ref, *refs, nseg, pairs, vh):
    o_ref = refs[-1]
    nt_dims = (((1,), (1,)), ((), ()))
    lane = lax.broadcasted_iota(jnp.int32, (q_ref.shape[0], LANE), 1)
    for p in range(pairs):
        outs = []
        for hh in range(2):
            c0 = (2 * p + hh) * LANE
            q = q_ref[:, c0:c0 + LANE]
            m = l = acc = None
            for i in range(nseg):
                k_ref, v_ref = refs[2 * i], refs[2 * i + 1]
                kc = min(k_ref.shape[0], KEY_CHUNK)
                for c in range(k_ref.shape[0] // kc):
                    rows = slice(c * kc, (c + 1) * kc)
                    s = lax.dot_general(q, k_ref[rows, c0:c0 + LANE], nt_dims, preferred_element_type=F32)
                    v = v_ref[rows, p * LANE:(p + 1) * LANE]
                    ms = jnp.max(s, axis=-1, keepdims=True)
                    if m is None:
                        m = ms
                        e = jnp.exp(s - m)
                        l = jnp.sum(e, axis=-1, keepdims=True)
                        acc = jnp.dot(e.astype(BF16), v, preferred_element_type=F32)
                    else:
                        m_new = jnp.maximum(m, ms)
                        alpha = jnp.exp(m - m_new)
                        e = jnp.exp(s - m_new)
                        l = alpha * l + jnp.sum(e, axis=-1, keepdims=True)
                        acc = alpha * acc + jnp.dot(e.astype(BF16), v, preferred_element_type=F32)
                        m = m_new
            outs.append(acc / l)
        o_ref[:, p * LANE:(p + 1) * LANE] = jnp.where(lane < vh, outs[0], outs[1]).astype(BF16)


def attention(dm, q, segs, row0, nseq, length, tq, pairs, prev=None):
    npg = dm.heads // 2 // pairs
    nqt = length // tq
    qb0 = row0 // tq
    in_specs = [pl.BlockSpec((tq, pairs * 2 * LANE), lambda s, g, t: (qb0 + s * nqt + t, g))]
    args = [q]
    for (k, v, lk, r0) in segs:
        kb0 = r0 // lk
        in_specs.append(pl.BlockSpec((lk, pairs * 2 * LANE), lambda s, g, t, kb0=kb0: (kb0 + s, g)))
        in_specs.append(pl.BlockSpec((lk, pairs * LANE), lambda s, g, t, kb0=kb0: (kb0 + s, g)))
        args += [k, v]
    aliases = {}
    if prev is not None:
        in_specs.append(pl.BlockSpec(memory_space=pl.ANY))
        args.append(prev)
        aliases = {len(args) - 1: 0}
    return pl.pallas_call(
        functools.partial(_attn_kernel, nseg=len(segs), pairs=pairs, vh=dm.vh),
        out_shape=jax.ShapeDtypeStruct((dm.nt, dm.heads * dm.vh), BF16),
        grid=(nseq, npg, nqt),
        in_specs=in_specs,
        out_specs=pl.BlockSpec((tq, pairs * LANE), lambda s, g, t: (qb0 + s * nqt + t, g)),
        input_output_aliases=aliases,
        compiler_params=_cparams("arbitrary", "arbitrary", "arbitrary"),
        name="attention_%d" % length,
    )(*args)


def _conf_kernel(a_ref, g_ref, w_ref, b_ref, lg_ref, lb_ref, *rest, length, ktaps, sub):
    o_ref, hs_ref, sh_ref, w8_ref = rest[-4:]
    cw = a_ref.shape[1]
    off = HALO - (ktaps - 1) // 2
    span = sub + SUBLANE * ((off + ktaps - 1) // SUBLANE)
    groups = CONV_CHUNK // SUBLANE
    hs_ref[0:HALO, :] = jnp.zeros((HALO, cw), F32)
    hs_ref[HALO + length:, :] = jnp.zeros((HALO, cw), F32)
    hs_ref[HALO:HALO + length, :] = a_ref[...].astype(F32) * jax.nn.sigmoid(g_ref[...].astype(F32))
    for j in range(ktaps):
        w8_ref[j] = jnp.broadcast_to(w_ref[j:j + 1, :], (SUBLANE, cw))
    bias, lg, lb = b_ref[...], lg_ref[...], lb_ref[...]

    for sb in range(length // sub):
        base = sb * sub
        for r in range(SUBLANE):
            sh_ref[r] = hs_ref[base + r:base + r + span, :]

        def body(c, carry):
            t0 = pl.multiple_of(c * CONV_CHUNK, CONV_CHUNK)
            acc = jnp.broadcast_to(bias, (groups, SUBLANE, cw))
            for j in range(ktaps):
                q, r = divmod(off + j, SUBLANE)
                xj = sh_ref[r, pl.ds(t0 + SUBLANE * q, CONV_CHUNK), :].reshape(groups, SUBLANE, cw)
                acc = acc + w8_ref[j] * xj
            acc = acc.reshape(CONV_CHUNK, cw)
            mu = jnp.mean(acc, axis=-1, keepdims=True)
            xc = acc - mu
            var = jnp.mean(xc * xc, axis=-1, keepdims=True)
            y = xc * lax.rsqrt(var + EPS) * lg + lb
            o_ref[pl.ds(base + t0, CONV_CHUNK), :] = _silu(y).astype(BF16)
            return carry

        lax.fori_loop(0, sub // CONV_CHUNK, body, 0, unroll=2)


def conformer(dm, p, cf_col0, row0, nseq, length, w, b, lg, lb, prev=None):
    cw = dm.cw
    rb0 = row0 // length
    cb0 = cf_col0 // cw
    one = lambda s: (0, 0)
    in_specs = [pl.BlockSpec((length, cw), lambda s: (rb0 + s, cb0)),
                pl.BlockSpec((length, cw), lambda s: (rb0 + s, cb0 + 1)),
                pl.BlockSpec((dm.cf_k, cw), one), pl.BlockSpec((1, cw), one),
                pl.BlockSpec((1, cw), one), pl.BlockSpec((1, cw), one)]
    args = [p, p, w, b, lg, lb]
    aliases = {}
    if prev is not None:
        in_specs.append(pl.BlockSpec(memory_space=pl.ANY))
        args.append(prev)
        aliases = {len(args) - 1: 0}
    sub = min(length, CONV_SUB)
    span = sub + SUBLANE * ((HALO + (dm.cf_k - 1) // 2) // SUBLANE)
    return pl.pallas_call(
        functools.partial(_conf_kernel, length=length, ktaps=dm.cf_k, sub=sub),
        out_shape=jax.ShapeDtypeStruct((dm.nt, cw), BF16),
        grid=(nseq,),
        in_specs=in_specs,
        out_specs=pl.BlockSpec((length, cw), lambda s: (rb0 + s, 0)),
        scratch_shapes=[pltpu.VMEM((length + 2 * HALO, cw), F32),
                        pltpu.VMEM((SUBLANE, span, cw), F32),
                        pltpu.VMEM((dm.cf_k, SUBLANE, cw), F32)],
        input_output_aliases=aliases,
        compiler_params=_cparams("arbitrary"),
        name="conformer_%d" % length,
    )(*args)


def _merge_kernel(z_ref, o_ref, h_ref, ga_ref, gb_ref, gc_ref, wa_ref, wb_ref, wc_ref, bc_ref, m_ref):
    ya = jnp.dot(z_ref[...], wa_ref[...], preferred_element_type=F32)
    yb = jnp.dot(o_ref[...], wb_ref[...], preferred_element_type=F32)
    yc = jnp.dot(h_ref[...], wc_ref[...], preferred_element_type=F32) + bc_ref[...]
    sg = lambda r: jax.nn.sigmoid(r[...].astype(F32))
    m_ref[...] = (sg(ga_ref) * ya + sg(gb_ref) * yb + sg(gc_ref) * yc).astype(BF16)


def merge(dm, z2, o, hc, p, gate_col0, wa, wb, wc, bc):
    tm, d = dm.tm_s, dm.d
    gb0 = gate_col0 // d
    row = lambda i: (i, 0)
    one = lambda i: (0, 0)
    return pl.pallas_call(
        _merge_kernel,
        out_shape=jax.ShapeDtypeStruct((dm.nt, d), BF16),
        grid=(dm.nt // tm,),
        in_specs=[pl.BlockSpec((tm, dm.hw), row), pl.BlockSpec((tm, dm.heads * dm.vh), row),
                  pl.BlockSpec((tm, dm.cw), row),
                  pl.BlockSpec((tm, d), lambda i: (i, gb0)), pl.BlockSpec((tm, d), lambda i: (i, gb0 + 1)),
                  pl.BlockSpec((tm, d), lambda i: (i, gb0 + 2)),
                  _resident((dm.hw, d), one), _resident((dm.heads * dm.vh, d), one),
                  _resident((dm.cw, d), one), pl.BlockSpec((1, d), one)],
        out_specs=pl.BlockSpec((tm, d), row),
        compiler_params=_cparams("arbitrary"),
        name="merge",
    )(z2, o, hc, p, p, p, wa, wb, wc, bc)


def _outproj_kernel(m_ref, w_ref, x_ref, mod_ref, g_ref, wr_ref, br_ref, x1_ref, h2_ref, lg_ref, *, d):
    y = jnp.dot(m_ref[...], w_ref[...], preferred_element_type=F32)
    g1 = mod_ref[0, :, 2 * d:3 * d]
    sh2 = mod_ref[0, :, 3 * d:4 * d]
    sc2 = mod_ref[0, :, 4 * d:5 * d]
    x1 = x_ref[...] + g1 * y
    x1_ref[...] = x1
    h2 = _rms(x1, g_ref[...]) * (1.0 + sc2) + sh2
    h2_ref[...] = _pack_bf16_pairs(h2)
    lg_ref[...] = jnp.dot(h2.astype(BF16), wr_ref[...], preferred_element_type=F32) + br_ref[...]


def outproj(dm, merged, w_out, x, mod_l, g, wr, br):
    tm, d = dm.tm_s, dm.d
    cid = _cond_id(dm, tm)
    row = lambda i: (i, 0)
    one = lambda i: (0, 0)
    return pl.pallas_call(
        functools.partial(_outproj_kernel, d=d),
        out_shape=(jax.ShapeDtypeStruct((dm.nt, d), F32), jax.ShapeDtypeStruct((dm.nt, d // 2), U32),
                   jax.ShapeDtypeStruct((dm.nt, LANE), F32)),
        grid=(dm.nt // tm,),
        in_specs=[pl.BlockSpec((tm, d), row), _resident((d, d), one), pl.BlockSpec((tm, d), row),
                  pl.BlockSpec((1, 1, 6 * d), lambda i: (cid(i), 0, 0)), pl.BlockSpec((1, d), one),
                  pl.BlockSpec((d, LANE), one), pl.BlockSpec((1, LANE), one)],
        out_specs=(pl.BlockSpec((tm, d), row), pl.BlockSpec((tm, d // 2), row), pl.BlockSpec((tm, LANE), row)),
        compiler_params=_cparams("arbitrary"),
        name="outproj",
    )(merged, w_out, x, mod_l, g, wr, br)


def _route_kernel(lg_ref, info_ref, cnt_ref, carry_ref, *, n_exp, groups, epg):
    @pl.when(pl.program_id(0) == 0)
    def _():
        carry_ref[...] = jnp.zeros_like(carry_ref)

    x = lg_ref[...]
    lane = lax.broadcasted_iota(jnp.int32, x.shape, 1).astype(F32)
    big = jnp.float32(1e9)
    neg = jnp.float32(-jnp.inf)
    is_g = (lane >= n_exp) & (lane < n_exp + groups)
    xg = jnp.where(is_g, x, neg)
    mg = jnp.max(xg, axis=-1, keepdims=True)
    sg = jnp.sum(jnp.where(is_g, jnp.exp(xg - mg), 0.0), axis=-1, keepdims=True)
    pg_top = 1.0 / sg
    gidx = jnp.min(jnp.where(xg == mg, lane, big), axis=-1, keepdims=True) - n_exp
    lo = gidx * epg
    in_grp = (lane >= lo) & (lane < lo + epg)
    xe = jnp.where(in_grp, x, neg)
    m1 = jnp.max(xe, axis=-1, keepdims=True)
    e1 = jnp.min(jnp.where(xe == m1, lane, big), axis=-1, keepdims=True)
    xe2 = jnp.where(lane == e1, neg, xe)
    m2 = jnp.max(xe2, axis=-1, keepdims=True)
    e2 = jnp.min(jnp.where(xe2 == m2, lane, big), axis=-1, keepdims=True)
    t = jnp.exp(m2 - m1)
    w1 = pg_top / (1.0 + t)
    w2 = pg_top * t / (1.0 + t)
    tm = x.shape[0]
    o1 = jnp.where(lane == e1, 1.0, 0.0)
    o2 = jnp.where(lane == e2, 1.0, 0.0)
    rr = lax.broadcasted_iota(jnp.int32, (tm, tm), 0)
    cc = lax.broadcasted_iota(jnp.int32, (tm, tm), 1)
    tri = jnp.where(cc < rr, 1.0, 0.0).astype(BF16)
    cum1 = jnp.dot(tri, o1.astype(BF16), preferred_element_type=F32)
    cum2 = jnp.dot(tri, o2.astype(BF16), preferred_element_type=F32)
    tot1 = jnp.sum(o1, axis=0, keepdims=True)
    tot2 = jnp.sum(o2, axis=0, keepdims=True)
    carry = carry_ref[0:1, :]
    rank1 = jnp.sum(o1 * (carry + cum1), axis=-1, keepdims=True)
    rank2 = jnp.sum(o2 * (carry + tot1 + cum2), axis=-1, keepdims=True)
    new = jnp.broadcast_to(carry + tot1 + tot2, carry_ref.shape)
    carry_ref[...] = new
    cnt_ref[...] = new
    cols = (e1, e2, rank1, rank2, w1, w2)
    info = jnp.zeros_like(x)
    for k, col in enumerate(cols):
        info = jnp.where(lane == k, col, info)
    info_ref[...] = info


def route(dm, logits):
    tm = dm.tm_s
    return pl.pallas_call(
        functools.partial(_route_kernel, n_exp=dm.n_exp, groups=dm.groups, epg=dm.epg),
        out_shape=(jax.ShapeDtypeStruct((dm.nt, LANE), F32), jax.ShapeDtypeStruct((SUBLANE, LANE), F32)),
        grid=(dm.nt // tm,),
        in_specs=[pl.BlockSpec((tm, LANE), lambda i: (i, 0))],
        out_specs=(pl.BlockSpec((tm, LANE), lambda i: (i, 0)), pl.BlockSpec((SUBLANE, LANE), lambda i: (0, 0))),
        scratch_shapes=[pltpu.VMEM((SUBLANE, LANE), F32)],
        compiler_params=_cparams("arbitrary"),
        name="route",
    )(logits)


def _positions_kernel(info_ref, cnt_ref, pos_ref, meta_ref, *, n_exp, tile):
    cnt = cnt_ref[0:1, :]
    padded = jnp.floor((cnt + (tile - 1.0)) * (1.0 / tile)) * tile
    a = lax.broadcasted_iota(jnp.int32, (LANE, LANE), 0)
    b = lax.broadcasted_iota(jnp.int32, (LANE, LANE), 1)
    before = jnp.where(a < b, 1.0, 0.0).astype(BF16)
    off = jnp.dot(jnp.broadcast_to(padded, (SUBLANE, LANE)).astype(BF16), before,
                  preferred_element_type=F32)[0:1, :]
    info = info_ref[...]
    lane = lax.broadcasted_iota(jnp.int32, info.shape, 1).astype(F32)
    e1, e2, r1, r2 = info[:, 0:1], info[:, 1:2], info[:, 2:3], info[:, 3:4]
    p1 = r1 + jnp.sum(jnp.where(lane == e1, off, 0.0), axis=-1, keepdims=True)
    p2 = r2 + jnp.sum(jnp.where(lane == e2, off, 0.0), axis=-1, keepdims=True)
    pos_ref[...] = jnp.where(lane == 0, p1, jnp.where(lane == 1, p2, 0.0)).astype(jnp.int32)

    end = off + padded
    end_col = jnp.sum(jnp.where(a == b, jnp.broadcast_to(end, (LANE, LANE)), 0.0), axis=1, keepdims=True)
    start = lax.broadcasted_iota(jnp.int32, (1, 2 * LANE), 1).astype(F32) * tile
    te = jnp.sum(jnp.where(end_col <= start, 1.0, 0.0), axis=0, keepdims=True)
    te = jnp.minimum(te, n_exp - 1.0)
    n_used = jnp.sum(padded, axis=-1, keepdims=True) * (1.0 / tile)
    tail = jnp.maximum(end - tile, 0.0)
    meta_ref[...] = jnp.zeros_like(meta_ref)
    meta_ref[0:1, :] = te.astype(jnp.int32)
    meta_ref[1:2, 0:LANE] = tail.astype(jnp.int32)
    meta_ref[2:3, :] = jnp.broadcast_to(n_used, (1, 2 * LANE)).astype(jnp.int32)


def positions(dm, info, cnt):
    tm = dm.tm_s
    return pl.pallas_call(
        functools.partial(_positions_kernel, n_exp=dm.n_exp, tile=dm.moe_tile),
        out_shape=(jax.ShapeDtypeStruct((dm.nt, LANE), jnp.int32),
                   jax.ShapeDtypeStruct((SUBLANE, 2 * LANE), jnp.int32)),
        grid=(dm.nt // tm,),
        in_specs=[pl.BlockSpec((tm, LANE), lambda i: (i, 0)), pl.BlockSpec((SUBLANE, LANE), lambda i: (0, 0))],
        out_specs=(pl.BlockSpec((tm, LANE), lambda i: (i, 0)),
                   pl.BlockSpec((SUBLANE, 2 * LANE), lambda i: (0, 0))),
        compiler_params=_cparams("arbitrary"),
        name="positions",
    )(info, cnt)


def _row_copy(src, src_row, dst, dst_row, sem):
    return pltpu.make_async_copy(src.at[pl.ds(src_row, 1)], dst.at[pl.ds(dst_row, 1)], sem)


def _dispatch_kernel(tail_ref, pos1_ref, pos2_ref, h_ref, xs_ref, zero_ref, sem, zsem, *, n_exp, tile, tb):
    i = pl.program_id(0)

    def zero_copy(e):
        start = pl.multiple_of(tail_ref[e], SUBLANE)
        return pltpu.make_async_copy(zero_ref, xs_ref.at[pl.ds(start, tile)], zsem)

    @pl.when(i == 0)
    def _():
        zero_ref[...] = jnp.zeros_like(zero_ref)
        for e in range(n_exp):
            zero_copy(e).start()
        for e in range(n_exp):
            zero_copy(e).wait()

    def body(t, carry):
        _row_copy(h_ref, t, xs_ref, pos1_ref[0, 0, t], sem).start()
        _row_copy(h_ref, t, xs_ref, pos2_ref[0, 0, t], sem).start()
        return carry

    lax.fori_loop(0, tb, body, 0, unroll=8)
    pltpu.make_async_copy(xs_ref.at[pl.ds(0, 2 * tb)], xs_ref.at[pl.ds(0, 2 * tb)], sem).wait()


def dispatch(dm, tail, pos1, pos2, h2p):
    tb, tile = dm.tm_s, dm.moe_tile
    rows = 2 * dm.nt + dm.n_exp * tile
    blk = pl.BlockSpec((1, 1, tb), lambda i, tail: (i, 0, 0), memory_space=pltpu.SMEM)
    return pl.pallas_call(
        functools.partial(_dispatch_kernel, n_exp=dm.n_exp, tile=tile, tb=tb),
        out_shape=jax.ShapeDtypeStruct((rows, dm.d // 2), U32),
        grid_spec=pltpu.PrefetchScalarGridSpec(
            num_scalar_prefetch=1, grid=(dm.nt // tb,),
            in_specs=[blk, blk, pl.BlockSpec((tb, dm.d // 2), lambda i, tail: (i, 0))],
            out_specs=pl.BlockSpec(memory_space=pl.ANY),
            scratch_shapes=[pltpu.VMEM((tile, dm.d // 2), U32), pltpu.SemaphoreType.DMA,
                            pltpu.SemaphoreType.DMA]),
        compiler_params=_cparams("arbitrary"),
        name="dispatch",
    )(tail, pos1, pos2, h2p)


def _experts_kernel(te_ref, nu_ref, x_ref, wg_ref, wu_ref, wd_ref, y_ref):
    @pl.when(pl.program_id(0) < nu_ref[0])
    def _():
        x = _unpack_bf16_pairs(x_ref[...]).astype(BF16)
        hg = jnp.dot(x, wg_ref[0], preferred_element_type=F32)
        hu = jnp.dot(x, wu_ref[0], preferred_element_type=F32)
        a = (_silu(hg) * hu).astype(BF16)
        y_ref[...] = _pack_bf16_pairs(jnp.dot(a, wd_ref[0], preferred_element_type=F32))


def experts(dm, te, n_used, xs, wg, wu, wd):
    tile, d, fe = dm.moe_tile, dm.d, dm.fe
    rows = xs.shape[0]
    last = lambda r, nu: jnp.minimum(r, nu[0] - 1)
    return pl.pallas_call(
        _experts_kernel,
        out_shape=jax.ShapeDtypeStruct((rows, d // 2), U32),
        grid_spec=pltpu.PrefetchScalarGridSpec(
            num_scalar_prefetch=2, grid=(rows // tile,),
            in_specs=[pl.BlockSpec((tile, d // 2), lambda r, te, nu: (last(r, nu), 0)),
                      pl.BlockSpec((1, d, fe), lambda r, te, nu: (te[last(r, nu)], 0, 0)),
                      pl.BlockSpec((1, d, fe), lambda r, te, nu: (te[last(r, nu)], 0, 0)),
                      pl.BlockSpec((1, fe, d), lambda r, te, nu: (te[last(r, nu)], 0, 0))],
            out_specs=pl.BlockSpec((tile, d // 2), lambda r, te, nu: (last(r, nu), 0))),
        compiler_params=_cparams("arbitrary"),
        name="experts",
    )(te, n_used, xs, wg, wu, wd)


def _combine_kernel(pos1_ref, pos2_ref, ys_ref, info_ref, x1_ref, mod_ref, o_ref, b1_ref, b2_ref, sem,
                    *, d, tb):
    def body(t, carry):
        _row_copy(ys_ref, pos1_ref[0, 0, t], b1_ref, t, sem).start()
        _row_copy(ys_ref, pos2_ref[0, 0, t], b2_ref, t, sem).start()
        return carry

    lax.fori_loop(0, tb, body, 0, unroll=8)
    pltpu.make_async_copy(ys_ref.at[pl.ds(0, tb)], b1_ref, sem).wait()
    pltpu.make_async_copy(ys_ref.at[pl.ds(0, tb)], b2_ref, sem).wait()
    info = info_ref[...]
    y = info[:, 4:5] * _unpack_bf16_pairs(b1_ref[...]) + info[:, 5:6] * _unpack_bf16_pairs(b2_ref[...])
    o_ref[...] = x1_ref[...] + mod_ref[0, :, 5 * d:6 * d] * y


def combine(dm, pos1, pos2, ys, info, x1, mod_l):
    tb, d = dm.tm_s, dm.d
    cid = _cond_id(dm, tb)
    blk = pl.BlockSpec((1, 1, tb), lambda i: (i, 0, 0), memory_space=pltpu.SMEM)
    row = lambda i: (i, 0)
    return pl.pallas_call(
        functools.partial(_combine_kernel, d=d, tb=tb),
        out_shape=jax.ShapeDtypeStruct((dm.nt, d), F32),
        grid=(dm.nt // tb,),
        in_specs=[blk, blk, pl.BlockSpec(memory_space=pl.ANY), pl.BlockSpec((tb, LANE), row),
                  pl.BlockSpec((tb, d), row), pl.BlockSpec((1, 1, 6 * d), lambda i: (cid(i), 0, 0))],
        out_specs=pl.BlockSpec((tb, d), row),
        scratch_shapes=[pltpu.VMEM((tb, d // 2), U32), pltpu.VMEM((tb, d // 2), U32),
                        pltpu.SemaphoreType.DMA],
        compiler_params=_cparams("arbitrary"),
        name="combine",
    )(pos1, pos2, ys, info, x1, mod_l)


def _dft_tables(length):
    k = lax.broadcasted_iota(jnp.int32, (length, length), 0)
    s = lax.broadcasted_iota(jnp.int32, (length, length), 1)
    ang = ((k * s) % (2 * length)).astype(F32) * (math.pi / length)
    return jnp.cos(ang).astype(BF16), jnp.sin(ang).astype(BF16)


def _head_layout(dm, nope, rope):
    half = dm.rope // 2
    a = LANE // 2 - half
    ref = nope if nope is not None else rope
    z = lambda n: jnp.zeros(ref.shape[:-1] + (n,), ref.dtype)
    r1, r2 = (rope[..., :half], rope[..., half:]) if rope is not None else (z(half), z(half))
    n1, n2 = (nope[..., :a], nope[..., a:]) if nope is not None else (z(a), z(dm.nope - a))
    return jnp.concatenate([r1, n1, r2, n2, z(LANE - dm.dk)], axis=-1)


def _rope_tables(dm):
    length = dm.dec_seq
    n_freq = dm.rope // 4
    pos = jnp.arange(length, dtype=jnp.int32)
    row = (pos // dm.grid_w).astype(F32)
    col = (pos % dm.grid_w).astype(F32)
    inv = jnp.power(ROPE_BASE, -jnp.arange(n_freq, dtype=F32) / n_freq)
    ang = jnp.concatenate([row[:, None] * inv[None, :], col[:, None] * inv[None, :]], axis=-1)
    cos, sin = jnp.cos(ang), jnp.sin(ang)
    cos_t = _head_layout(dm, jnp.ones((length, dm.nope), F32), jnp.concatenate([cos, cos], axis=1))
    sin_t = _head_layout(dm, None, jnp.concatenate([-sin, sin], axis=1))
    return cos_t, sin_t


def _pack_w_in(dm, w_in):
    o = 0
    parts = {}
    for name, n in (("hy", 3 * dm.hw), ("cq", dm.q_lora), ("ckv", dm.kv_lora), ("kpe", dm.rope),
                    ("cf", 2 * dm.cw), ("gate", 3 * dm.d)):
        parts[name] = w_in[:, o:o + n]
        o += n
    used = dm.q_lora + dm.kv_lora + LANE
    small = [parts["cq"], parts["ckv"], _head_layout(dm, None, parts["kpe"]),
             jnp.zeros((w_in.shape[0], dm.tn - used), w_in.dtype)]
    return jnp.concatenate(small + [parts["gate"], parts["hy"], parts["cf"]], axis=1).astype(BF16)


def _forward(dm, x_prompt, x_sample, cache_ckv, cache_kpe, c, c_ctx, w_mod, b_mod, norm_mix, norm_ffn,
             w_in, hy_short, hy_f_w1, hy_f_b1, hy_f_w2, hy_f_b2, hy_f_w3, hy_decay, hy_bias, w_hy_out,
             q_a_norm, w_uq, kv_a_norm, w_ukv, q_norm, k_norm, w_mla_out,
             cf_dw, cf_dw_b, cf_ln_g, cf_ln_b, w_cf_out, b_cf_out, w_out,
             w_rg, b_rg, w_re, b_re, w_eg, w_eu, w_ed):
    d, hw, cw, heads = dm.d, dm.hw, dm.cw, dm.heads
    n_ctx, nt = dm.n_ctx, dm.nt
    gate_col0, hy_col0, cf_col0 = 0, 3 * d, 3 * d + 3 * hw
    ckv_blk = dm.q_lora // dm.kv_lora
    kpe_blk = (dm.q_lora + dm.kv_lora) // LANE
    kpe_lane0 = dm.q_lora + dm.kv_lora
    half = dm.rope // 2

    x = jnp.concatenate([x_prompt.reshape(n_ctx, d), x_sample.reshape(dm.n_lat, d)], axis=0)
    cond8 = jnp.concatenate([c_ctx[None, :], c, jnp.zeros((8 - 1 - dm.dec_batch, d), F32)], axis=0)
    mod = adaln(dm, cond8, w_mod, b_mod)

    dft = {dm.seq: _dft_tables(dm.seq), dm.dec_seq: _dft_tables(dm.dec_seq)}
    cos_t, sin_t = _rope_tables(dm)
    scale = float(dm.dk) ** -0.5
    tc_ctx = min(hw, 1024)
    tq_ctx = dm.seq
    pairs_ctx = heads // 2
    tm_cache = min(dm.dec_batch * dm.past, dm.tm_s)

    ckv_out, kpe_out = [], []
    for l in range(dm.depth):
        mod_l = mod[l][:, None, :]
        w_packed = _pack_w_in(dm, w_in[l])
        p2, p = inproj(dm, x, mod_l, norm_mix[l][None, :], w_packed)

        w1p = jnp.zeros((LANE, dm.hy_fhid), F32).at[:1 + 2 * dm.hy_freqs].set(hy_f_w1[l])
        z2 = None
        for (length, row0, nseq, tc, nb) in ((dm.seq, 0, dm.batch, tc_ctx, 1),
                                             (dm.dec_seq, n_ctx, dm.dec_batch, min(hw, dm.tc_hy), dm.hy_nb)):
            cmat, smat = dft[length]
            filt = hyena_filters(dm, length, w1p, hy_f_b1[l][None, :], hy_f_w2[l], hy_f_b2[l][None, :],
                                 hy_f_w3[l], hy_decay[l][None, :])
            kc, ks, kn = hyena_spectra(dm, length, filt, cmat, smat)
            z2 = hyena_conv(dm, p, hy_col0, row0, nseq, length, tc, nb, hy_short[l], hy_bias[l], kc, ks, kn,
                            cmat, smat, prev=z2)

        wq = w_uq[l].reshape(dm.q_lora, heads, dm.dk)
        w_uq_p = _head_layout(dm, wq[..., :dm.nope], wq[..., dm.nope:])
        w_uq_p = w_uq_p.reshape(dm.q_lora, heads * LANE).astype(BF16)
        wkv = w_ukv[l].reshape(dm.kv_lora, heads, dm.nope + dm.vh)
        wk_p = _head_layout(dm, wkv[:, :, :dm.nope], None)
        w_ukv_p = jnp.concatenate([wk_p.reshape(dm.kv_lora, heads * LANE),
                                   wkv[:, :, dm.nope:].reshape(dm.kv_lora, heads * dm.vh)], axis=1).astype(BF16)
        gq = _head_layout(dm, q_norm[l][:dm.nope] * scale, q_norm[l][dm.nope:] * scale)[None, :]
        gk = _head_layout(dm, k_norm[l][:dm.nope], k_norm[l][dm.nope:])[None, :]
        q = mla_q(dm, p2, q_a_norm[l][None, :], w_uq_p, gq, cos_t, sin_t)
        k, v, ckv_n = mla_kv(dm, p2, ckv_blk, p2, kpe_blk, nt, dm.tm_s, kv_a_norm[l][None, :], w_ukv_p, gk,
                             rope=(cos_t, sin_t))
        kpe_c = _head_layout(dm, None, cache_kpe[:, l].reshape(dm.dec_batch * dm.past, dm.rope))
        k_c, v_c, _ = mla_kv(dm, cache_ckv[:, l].reshape(dm.dec_batch * dm.past, dm.kv_lora), 0, kpe_c, 0,
                             dm.dec_batch * dm.past, tm_cache, kv_a_norm[l][None, :], w_ukv_p, gk)
        o = attention(dm, q, [(k, v, dm.seq, 0)], 0, dm.batch, dm.seq, tq_ctx, pairs_ctx)
        o = attention(dm, q, [(k, v, dm.dec_seq, n_ctx), (k_c, v_c, dm.past, 0)], n_ctx, dm.dec_batch,
                      dm.dec_seq, min(dm.tq, dm.dec_seq), 1, prev=o)
        ckv_out.append(ckv_n[:n_ctx].reshape(dm.batch, dm.seq, dm.kv_lora))
        kpe_blk_ctx = p2[:n_ctx, kpe_lane0:kpe_lane0 + LANE]
        kpe_out.append(jnp.concatenate([kpe_blk_ctx[:, :half], kpe_blk_ctx[:, LANE // 2:LANE // 2 + half]],
                                       axis=1).reshape(dm.batch, dm.seq, dm.rope))

        hc = conformer(dm, p, cf_col0, 0, dm.batch, dm.seq, cf_dw[l], cf_dw_b[l][None, :],
                       cf_ln_g[l][None, :], cf_ln_b[l][None, :])
        hc = conformer(dm, p, cf_col0, n_ctx, dm.dec_batch, dm.dec_seq, cf_dw[l], cf_dw_b[l][None, :],
                       cf_ln_g[l][None, :], cf_ln_b[l][None, :], prev=hc)

        merged = merge(dm, z2, o, hc, p, gate_col0, w_hy_out[l].astype(BF16), w_mla_out[l].astype(BF16),
                       w_cf_out[l].astype(BF16), b_cf_out[l][None, :])
        wr = jnp.concatenate([w_re[l], w_rg[l], jnp.zeros((d, LANE - dm.n_exp - dm.groups), F32)],
                             axis=1).astype(BF16)
        br = jnp.concatenate([b_re[l], b_rg[l], jnp.zeros((LANE - dm.n_exp - dm.groups,), F32)])[None, :]
        x1, h2p, logits = outproj(dm, merged, w_out[l].astype(BF16), x, mod_l, norm_ffn[l][None, :], wr, br)

        info, cnt = route(dm, logits)
        pos, meta = positions(dm, info, cnt)
        pos1 = pos[:, 0].reshape(nt // dm.tm_s, 1, dm.tm_s)
        pos2 = pos[:, 1].reshape(nt // dm.tm_s, 1, dm.tm_s)
        n_tiles = (2 * nt + dm.n_exp * dm.moe_tile) // dm.moe_tile
        xs = dispatch(dm, meta[1, :dm.n_exp], pos1, pos2, h2p)
        ys = experts(dm, meta[0, :n_tiles], meta[2, :1], xs, w_eg[l].astype(BF16), w_eu[l].astype(BF16),
                     w_ed[l].astype(BF16))
        x = combine(dm, pos1, pos2, ys, info, x1, mod_l)

    y_prompt = x[:n_ctx].reshape(dm.batch, dm.seq, d)
    y_sample = x[n_ctx:].reshape(dm.dec_batch, dm.dec_seq, d)
    return y_prompt, y_sample, jnp.stack(ckv_out, axis=1), jnp.stack(kpe_out, axis=1)


def kernel(x_prompt, x_sample, cache_ckv, cache_kpe, c, c_ctx, w_mod, b_mod, norm_mix, norm_ffn, w_in, hy_short, hy_f_w1, hy_f_b1, hy_f_w2, hy_f_b2, hy_f_w3, hy_decay, hy_bias, w_hy_out, q_a_norm, w_uq, kv_a_norm, w_ukv, q_norm, k_norm, w_mla_out, cf_dw, cf_dw_b, cf_ln_g, cf_ln_b, w_cf_out, b_cf_out, w_out, w_rg, b_rg, w_re, b_re, w_eg, w_eu, w_ed):
    return _forward(Dims(), x_prompt, x_sample, cache_ckv, cache_kpe, c, c_ctx, w_mod, b_mod, norm_mix,
                    norm_ffn, w_in, hy_short, hy_f_w1, hy_f_b1, hy_f_w2, hy_f_b2, hy_f_w3, hy_decay, hy_bias,
                    w_hy_out, q_a_norm, w_uq, kv_a_norm, w_ukv, q_norm, k_norm, w_mla_out,
                    cf_dw, cf_dw_b, cf_ln_g, cf_ln_b, w_cf_out, b_cf_out, w_out,
                    w_rg, b_rg, w_re, b_re, w_eg, w_eu, w_ed)
```

```python
import functools
import math
from typing import NamedTuple

import jax
import jax.numpy as jnp
from jax import lax
from jax.experimental import pallas as pl
from jax.experimental.pallas import tpu as pltpu

F32 = jnp.float32
BF16 = jnp.bfloat16
U32 = jnp.uint32
EPS = 1e-6
LANE = 128
SUBLANE = 8
CONV_SUB = 256
KEY_CHUNK = 512
VMEM_LIMIT_BYTES = 56 << 20
ROPE_BASE = 10000.0
HY_ORDER = 2
HY_SHORT = 3
HALO = 16
CONV_CHUNK = 16


class Dims(NamedTuple):
    d: int = 2048
    batch: int = 32
    seq: int = 256
    depth: int = 4
    dec_batch: int = 4
    dec_seq: int = 2048
    past: int = 256
    grid_w: int = 64
    hw: int = 1024
    hy_freqs: int = 16
    hy_fhid: int = 64
    heads: int = 16
    nope: int = 64
    rope: int = 32
    vh: int = 64
    q_lora: int = 512
    kv_lora: int = 256
    cw: int = 1024
    cf_k: int = 31
    groups: int = 4
    epg: int = 4
    fe: int = 512
    tm: int = 1024
    tm_s: int = 512
    tn: int = 1024
    tq: int = 1024
    tc_hy: int = 256
    hy_nb: int = 2
    hy_rb: int = 1024
    moe_tile: int = 256

    @property
    def n_ctx(self):
        return self.batch * self.seq

    @property
    def n_lat(self):
        return self.dec_batch * self.dec_seq

    @property
    def nt(self):
        return self.n_ctx + self.n_lat

    @property
    def n_exp(self):
        return self.groups * self.epg

    @property
    def dk(self):
        return self.nope + self.rope


def _cparams(*sem):
    return pltpu.CompilerParams(dimension_semantics=sem, vmem_limit_bytes=VMEM_LIMIT_BYTES)


def _resident(shape, index_map):
    return pl.BlockSpec(shape, index_map, pipeline_mode=pl.Buffered(1))


def _silu(x):
    return x * jax.nn.sigmoid(x)


def _pack_bf16_pairs(x):
    n = x.shape[1] // 2
    lo = lax.bitcast_convert_type(x[:, :n].astype(jnp.bfloat16).astype(F32), U32) >> 16
    hi = lax.bitcast_convert_type(x[:, n:].astype(jnp.bfloat16).astype(F32), U32) & jnp.uint32(0xFFFF0000)
    return hi | lo


def _unpack_bf16_pairs(u):
    lo = lax.bitcast_convert_type(u << 16, F32)
    hi = lax.bitcast_convert_type(u & jnp.uint32(0xFFFF0000), F32)
    return jnp.concatenate([lo, hi], axis=1)


def _rms(x, g):
    return x * lax.rsqrt(jnp.mean(x * x, axis=-1, keepdims=True) + EPS) * g


def _cond_id(dm, tile):
    n_ctx_tiles = dm.n_ctx // tile
    per_seq = dm.dec_seq // tile
    return lambda i: jnp.where(i < n_ctx_tiles, 0, 1 + (i - n_ctx_tiles) // per_seq)


def _rope_blk(dm, tile):
    n_ctx_tiles = dm.n_ctx // tile
    per_seq = dm.dec_seq // tile
    return lambda i: jnp.where(i < n_ctx_tiles, 0, (i - n_ctx_tiles) % per_seq)


def _adaln_kernel(c_ref, w_ref, b_ref, o_ref):
    a = _silu(c_ref[...]).astype(BF16)
    o_ref[0] = jnp.dot(a, w_ref[0].astype(BF16), preferred_element_type=F32) + b_ref[0]


def adaln(dm, cond8, w_mod, b_mod):
    depth, d, n = w_mod.shape
    tn = min(n, 1024)
    return pl.pallas_call(
        _adaln_kernel,
        out_shape=jax.ShapeDtypeStruct((depth, 8, n), F32),
        grid=(depth, n // tn),
        in_specs=[pl.BlockSpec((8, d), lambda l, j: (0, 0)),
                  pl.BlockSpec((1, d, tn), lambda l, j: (l, 0, j)),
                  pl.BlockSpec((1, 1, tn), lambda l, j: (l, 0, j))],
        out_specs=pl.BlockSpec((1, 8, tn), lambda l, j: (l, 0, j)),
        compiler_params=_cparams("arbitrary", "arbitrary"),
        name="adaln",
    )(cond8, w_mod, b_mod.reshape(depth, 1, n))


def _inproj_kernel(x_ref, mod_ref, g_ref, w_ref, p2_ref, p_ref, hn_ref, *, d):
    j = pl.program_id(1)

    @pl.when(j == 0)
    def _():
        y = _rms(x_ref[...], g_ref[...])
        sh = mod_ref[0, :, 0:d]
        sc = mod_ref[0, :, d:2 * d]
        hn_ref[...] = (y * (1.0 + sc) + sh).astype(BF16)

    acc = jnp.dot(hn_ref[...], w_ref[...], preferred_element_type=F32)
    p_ref[...] = acc.astype(BF16)

    @pl.when(j == 0)
    def _():
        p2_ref[...] = acc


def inproj(dm, x, mod_l, g, w_packed):
    nt, d = x.shape
    tm, tn = dm.tm, dm.tn
    ncols = w_packed.shape[1]
    nj = ncols // tn
    cid = _cond_id(dm, tm)
    return pl.pallas_call(
        functools.partial(_inproj_kernel, d=d),
        out_shape=(jax.ShapeDtypeStruct((nt, tn), F32),
                   jax.ShapeDtypeStruct((nt, ncols - tn), BF16)),
        grid=(nt // tm, nj),
        in_specs=[pl.BlockSpec((tm, d), lambda i, j: (i, 0)),
                  pl.BlockSpec((1, 1, 6 * d), lambda i, j: (cid(i), 0, 0)),
                  pl.BlockSpec((1, d), lambda i, j: (0, 0)),
                  pl.BlockSpec((d, tn), lambda i, j: (0, j))],
        out_specs=(pl.BlockSpec((tm, tn), lambda i, j: (i, 0)),
                   pl.BlockSpec((tm, tn), lambda i, j: (i, jnp.maximum(j - 1, 0)))),
        scratch_shapes=[pltpu.VMEM((tm, d), BF16)],
        compiler_params=_cparams("arbitrary", "arbitrary"),
        name="inproj",
    )(x, mod_l, g, w_packed)


def _dot3(a, b):
    ah = a.astype(BF16)
    al = (a - ah.astype(F32)).astype(BF16)
    bh = b.astype(BF16)
    bl = (b - bh.astype(F32)).astype(BF16)
    return (jnp.dot(ah, bh, preferred_element_type=F32) + jnp.dot(al, bh, preferred_element_type=F32)
            + jnp.dot(ah, bl, preferred_element_type=F32))


def _hyfilt_kernel(w1_ref, b1_ref, w2_ref, b2_ref, w3_ref, dec_ref, f_ref, h_ref, *, length, nfreq):
    @pl.when(pl.program_id(0) == 0)
    def _():
        row = lax.broadcasted_iota(jnp.int32, (length, LANE), 0).astype(F32)
        lane = lax.broadcasted_iota(jnp.int32, (length, LANE), 1)
        tn = row / float(length)
        fr = jnp.where(lane <= nfreq, lane, lane - nfreq).astype(F32)
        ang = (2.0 * math.pi) * tn * fr
        feat = jnp.where(lane == 0, tn, jnp.where(lane <= nfreq, jnp.cos(ang),
                                                  jnp.where(lane <= 2 * nfreq, jnp.sin(ang), 0.0)))
        h = jnp.sin(_dot3(feat, w1_ref[...]) + b1_ref[...])
        h_ref[...] = jnp.sin(_dot3(h, w2_ref[...]) + b2_ref[...])

    tn = lax.broadcasted_iota(jnp.int32, (length, 1), 0).astype(F32) / float(length)
    f_ref[...] = _dot3(h_ref[...], w3_ref[...]) * jnp.exp(-tn * dec_ref[...])


def hyena_filters(dm, length, w1p, b1, w2, b2, w3, decay):
    ncol = w3.shape[1]
    tn = min(ncol, 1024)
    fh = w2.shape[0]
    return pl.pallas_call(
        functools.partial(_hyfilt_kernel, length=length, nfreq=dm.hy_freqs),
        out_shape=jax.ShapeDtypeStruct((length, ncol), F32),
        grid=(ncol // tn,),
        in_specs=[pl.BlockSpec((LANE, fh), lambda j: (0, 0)),
                  pl.BlockSpec((1, fh), lambda j: (0, 0)),
                  pl.BlockSpec((fh, fh), lambda j: (0, 0)),
                  pl.BlockSpec((1, fh), lambda j: (0, 0)),
                  pl.BlockSpec((fh, tn), lambda j: (0, j)),
                  pl.BlockSpec((1, tn), lambda j: (0, j))],
        out_specs=pl.BlockSpec((length, tn), lambda j: (0, j)),
        scratch_shapes=[pltpu.VMEM((length, fh), F32)],
        compiler_params=_cparams("arbitrary"),
        name="hyena_filters",
    )(w1p, b1, w2, b2, w3, decay)


def _hyspec_kernel(hf_ref, hb_ref, c_ref, s_ref, kc_ref, ks_ref, kn_ref, *, length):
    row = lax.broadcasted_iota(jnp.int32, (length, 1), 0)
    hf = hf_ref[...]
    hb0 = jnp.where(row == 0, 0.0, hb_ref[...])
    sp = hf + hb0
    sm = hf - hb0
    inv = 1.0 / length
    kc = jnp.dot(c_ref[...], sp.astype(BF16), preferred_element_type=F32)
    kc_ref[0] = kc * jnp.where(row == 0, 0.5 * inv, inv)
    ks_ref[0] = jnp.dot(s_ref[...], sm.astype(BF16), preferred_element_type=F32) * inv
    sgn = (1 - 2 * (row & 1)).astype(F32)
    kn = jnp.sum(sgn * sp, axis=0, keepdims=True) * (0.5 * inv)
    kn_ref[0] = jnp.broadcast_to(kn, kn_ref.shape[1:])


def hyena_spectra(dm, length, filt, cmat, smat):
    hw = dm.hw
    tc = min(hw, 512)
    nct = hw // tc
    shp = jax.ShapeDtypeStruct((HY_ORDER, length, hw), F32)
    return pl.pallas_call(
        functools.partial(_hyspec_kernel, length=length),
        out_shape=(shp, shp, jax.ShapeDtypeStruct((HY_ORDER, 8, hw), F32)),
        grid=(HY_ORDER, nct),
        in_specs=[pl.BlockSpec((length, tc), lambda n, j: (0, (2 * n) * nct + j)),
                  pl.BlockSpec((length, tc), lambda n, j: (0, (2 * n + 1) * nct + j)),
                  _resident((length, length), lambda n, j: (0, 0)),
                  _resident((length, length), lambda n, j: (0, 0))],
        out_specs=(pl.BlockSpec((1, length, tc), lambda n, j: (n, 0, j)),
                   pl.BlockSpec((1, length, tc), lambda n, j: (n, 0, j)),
                   pl.BlockSpec((1, 8, tc), lambda n, j: (n, 0, j))),
        compiler_params=_cparams("arbitrary", "arbitrary"),
        name="hyena_spectra",
    )(filt, filt, cmat, smat)


def _hyconv_kernel(v_ref, x1_ref, x2_ref, swv_ref, sw1_ref, sw2_ref, bias_ref, kc_ref, ks_ref, kn_ref,
                   c_ref, s_ref, *rest, length, nb, rb):
    o_ref, zb_ref, zc_ref, zs_ref, z1_ref = rest[-5:]
    tc = v_ref.shape[1]
    nblk = length // rb
    rowi = lax.broadcasted_iota(jnp.int32, (rb, 1), 0)
    sgn = (1 - 2 * (rowi & 1)).astype(F32)
    halo = 2 * SUBLANE

    def lanes(x):
        return jnp.concatenate([x] * nb, axis=1) if nb > 1 else x

    def short(x_ref, w_ref, q, r):
        base = q * length + r * rb
        x = x_ref[base:base + rb, :].astype(F32)
        prev = (jnp.zeros((1, tc), F32) if r == 0
                else x_ref[base - halo:base, :].astype(F32)[halo - 1:halo, :])
        nxt = (jnp.zeros((1, tc), F32) if r == nblk - 1
               else x_ref[base + rb:base + rb + halo, :].astype(F32)[0:1, :])
        xp = jnp.where(rowi == 0, prev, pltpu.roll(x, 1, 0))
        xn = jnp.where(rowi == rb - 1, nxt, pltpu.roll(x, rb - 1, 0))
        return w_ref[0:1, :] * xp + w_ref[1:2, :] * x + w_ref[2:3, :] * xn

    def forward(n):
        for r in range(nblk):
            rows = slice(r * rb, (r + 1) * rb)
            uc = jnp.dot(c_ref[rows, :], zb_ref[...], preferred_element_type=F32)
            us = jnp.dot(s_ref[rows, :], zb_ref[...], preferred_element_type=F32)
            kc = lanes(kc_ref[n, rows, :])
            ks = lanes(ks_ref[n, rows, :])
            zc_ref[rows, :] = (uc * kc - us * ks).astype(BF16)
            zs_ref[rows, :] = (uc * ks + us * kc).astype(BF16)

    def inverse(n, r, un):
        rows = slice(r * rb, (r + 1) * rb)
        y = jnp.dot(c_ref[rows, :], zc_ref[...], preferred_element_type=F32)
        y = y + jnp.dot(s_ref[rows, :], zs_ref[...], preferred_element_type=F32)
        return y + sgn * (un * lanes(kn_ref[n, 0:1, :]))

    un = []
    for q in range(nb):
        acc = jnp.zeros((1, tc), F32)
        for r in range(nblk):
            zv = short(v_ref, swv_ref, q, r)
            zb_ref[r * rb:(r + 1) * rb, q * tc:(q + 1) * tc] = zv.astype(BF16)
            acc = acc + jnp.sum(sgn * zv, axis=0, keepdims=True)
        un.append(acc)
    un = jnp.concatenate(un, axis=1) if nb > 1 else un[0]
    forward(0)
    un2 = [jnp.zeros((1, tc), F32) for _ in range(nb)]
    for r in range(nblk):
        rows = slice(r * rb, (r + 1) * rb)
        y = inverse(0, r, un)
        for q in range(nb):
            cols = slice(q * tc, (q + 1) * tc)
            zv = short(v_ref, swv_ref, q, r)
            z1 = short(x1_ref, sw1_ref, q, r) * (y[:, cols] + zv * bias_ref[0:1, :])
            z1_ref[rows, cols] = z1
            zb_ref[rows, cols] = z1.astype(BF16)
            un2[q] = un2[q] + jnp.sum(sgn * z1, axis=0, keepdims=True)
    un2 = jnp.concatenate(un2, axis=1) if nb > 1 else un2[0]
    forward(1)
    for r in range(nblk):
        rows = slice(r * rb, (r + 1) * rb)
        y = inverse(1, r, un2)
        for q in range(nb):
            cols = slice(q * tc, (q + 1) * tc)
            z2 = short(x2_ref, sw2_ref, q, r) * (y[:, cols] + z1_ref[rows, cols] * bias_ref[1:2, :])
            o_ref[q * length + r * rb:q * length + (r + 1) * rb, :] = z2.astype(BF16)


def hyena_conv(dm, p, hy_col0, row0, nseq, length, tc, nb, short_w, bias, kc, ks, kn, cmat, smat, prev=None):
    hw = dm.hw
    nct = hw // tc
    rows = nb * length
    rb0 = row0 // rows
    cb0 = hy_col0 // tc
    rb = min(length, dm.hy_rb)
    spec = _resident if rows * tc >= (1 << 20) else pl.BlockSpec

    def part(k):
        return spec((rows, tc), lambda j, s, k=k: (rb0 + s, cb0 + k * nct + j))

    def sw(k):
        return pl.BlockSpec((HY_SHORT, tc), lambda j, s, k=k: (0, k * nct + j))

    in_specs = [part(0), part(1), part(2), sw(0), sw(1), sw(2),
                pl.BlockSpec((HY_ORDER, tc), lambda j, s: (0, j)),
                _resident((HY_ORDER, length, tc), lambda j, s: (0, 0, j)),
                _resident((HY_ORDER, length, tc), lambda j, s: (0, 0, j)),
                pl.BlockSpec((HY_ORDER, 8, tc), lambda j, s: (0, 0, j)),
                _resident((length, length), lambda j, s: (0, 0)),
                _resident((length, length), lambda j, s: (0, 0))]
    args = [p, p, p, short_w, short_w, short_w, bias, kc, ks, kn, cmat, smat]
    aliases = {}
    if prev is not None:
        in_specs.append(pl.BlockSpec(memory_space=pl.ANY))
        args.append(prev)
        aliases = {len(args) - 1: 0}
    return pl.pallas_call(
        functools.partial(_hyconv_kernel, length=length, nb=nb, rb=rb),
        out_shape=jax.ShapeDtypeStruct((dm.nt, hw), BF16),
        grid=(nct, nseq // nb),
        in_specs=in_specs,
        out_specs=pl.BlockSpec((rows, tc), lambda j, s: (rb0 + s, j)),
        scratch_shapes=[pltpu.VMEM((length, nb * tc), BF16), pltpu.VMEM((length, nb * tc), BF16),
                        pltpu.VMEM((length, nb * tc), BF16), pltpu.VMEM((length, nb * tc), F32)],
        input_output_aliases=aliases,
        compiler_params=_cparams("arbitrary", "arbitrary"),
        name="hyena_conv_%d" % length,
    )(*args)


def _store_heads(x, gain, o_ref, rot, *, heads, dk):
    ones = jnp.ones((LANE, LANE), BF16)
    for h in range(heads):
        xh = x[:, h * LANE:(h + 1) * LANE]
        ss = jnp.dot((xh * xh).astype(BF16), ones, preferred_element_type=F32)
        xh = xh * lax.rsqrt(ss * (1.0 / dk) + EPS) * gain
        if rot is not None:
            xh = xh * rot[0] + pltpu.roll(xh, LANE // 2, 1) * rot[1]
        o_ref[:, h * LANE:(h + 1) * LANE] = xh.astype(BF16)


def _store_heads_by_segment(x, gain, o_ref, cos_ref, sin_ref, n_ctx_tiles, **kw):
    is_lat = pl.program_id(0) >= n_ctx_tiles

    @pl.when(is_lat)
    def _():
        _store_heads(x, gain, o_ref, (cos_ref[...], sin_ref[...]), **kw)

    @pl.when(jnp.logical_not(is_lat))
    def _():
        _store_heads(x, gain, o_ref, None, **kw)


def _q_kernel(cq_ref, ga_ref, w_ref, gh_ref, cos_ref, sin_ref, o_ref, *, heads, dk, n_ctx_tiles):
    y = _rms(cq_ref[...], ga_ref[...])
    q = jnp.dot(y.astype(BF16), w_ref[...], preferred_element_type=F32)
    _store_heads_by_segment(q, gh_ref[...], o_ref, cos_ref, sin_ref, n_ctx_tiles, heads=heads, dk=dk)


def mla_q(dm, p2, ga, w_uq, gh, cos_t, sin_t):
    tm = dm.tm_s
    rb = _rope_blk(dm, tm)
    tab = pl.BlockSpec((tm, LANE), lambda i: (rb(i), 0))
    return pl.pallas_call(
        functools.partial(_q_kernel, heads=dm.heads, dk=dm.dk, n_ctx_tiles=dm.n_ctx // tm),
        out_shape=jax.ShapeDtypeStruct((dm.nt, dm.heads * LANE), BF16),
        grid=(dm.nt // tm,),
        in_specs=[pl.BlockSpec((tm, dm.q_lora), lambda i: (i, 0)),
                  pl.BlockSpec((1, dm.q_lora), lambda i: (0, 0)),
                  pl.BlockSpec((dm.q_lora, dm.heads * LANE), lambda i: (0, 0)),
                  pl.BlockSpec((1, LANE), lambda i: (0, 0)),
                  tab, tab],
        out_specs=pl.BlockSpec((tm, dm.heads * LANE), lambda i: (i, 0)),
        compiler_params=_cparams("arbitrary"),
        name="mla_q",
    )(p2, ga, w_uq, gh, cos_t, sin_t)


def _kv_kernel(ckv_ref, kpe_ref, ga_ref, w_ref, gk_ref, *rest, heads, dk, n_ctx_tiles, from_cache):
    k_ref, v_ref, cn_ref = rest[-3:]
    x = ckv_ref[...]
    if not from_cache:
        x = _rms(x, ga_ref[...])
    cn_ref[...] = x
    kv = jnp.dot(x.astype(BF16), w_ref[...], preferred_element_type=F32)
    kpe = kpe_ref[...]
    k = kv[:, :heads * LANE] + jnp.concatenate([kpe] * heads, axis=1)
    if from_cache:
        _store_heads(k, gk_ref[...], k_ref, None, heads=heads, dk=dk)
    else:
        _store_heads_by_segment(k, gk_ref[...], k_ref, rest[0], rest[1], n_ctx_tiles, heads=heads, dk=dk)
    v_ref[...] = kv[:, heads * LANE:].astype(BF16)


def mla_kv(dm, ckv_src, ckv_blk, kpe_src, kpe_blk, nrows, tm, ga, w_ukv, gk, rope=None):
    hk = dm.heads * LANE
    hv = dm.heads * dm.vh
    in_specs = [pl.BlockSpec((tm, dm.kv_lora), lambda i: (i, ckv_blk)),
                pl.BlockSpec((tm, LANE), lambda i: (i, kpe_blk)),
                pl.BlockSpec((1, dm.kv_lora), lambda i: (0, 0)),
                pl.BlockSpec((dm.kv_lora, hk + hv), lambda i: (0, 0)),
                pl.BlockSpec((1, LANE), lambda i: (0, 0))]
    args = [ckv_src, kpe_src, ga, w_ukv, gk]
    if rope is not None:
        rb = _rope_blk(dm, tm)
        in_specs += [pl.BlockSpec((tm, LANE), lambda i: (rb(i), 0))] * 2
        args += list(rope)
    return pl.pallas_call(
        functools.partial(_kv_kernel, heads=dm.heads, dk=dm.dk, n_ctx_tiles=dm.n_ctx // tm,
                          from_cache=rope is None),
        out_shape=(jax.ShapeDtypeStruct((nrows, hk), BF16),
                   jax.ShapeDtypeStruct((nrows, hv), BF16),
                   jax.ShapeDtypeStruct((nrows, dm.kv_lora), F32)),
        grid=(nrows // tm,),
        in_specs=in_specs,
        out_specs=(pl.BlockSpec((tm, hk), lambda i: (i, 0)),
                   pl.BlockSpec((tm, hv), lambda i: (i, 0)),
                   pl.BlockSpec((tm, dm.kv_lora), lambda i: (i, 0))),
        compiler_params=_cparams("arbitrary"),
        name="mla_kv_trunk" if rope is not None else "mla_kv_cache",
    )(*args)


def _attn_kernel(q_ref, *refs, nseg, pairs, vh):
    o_ref = refs[-1]
    nt_dims = (((1,), (1,)), ((), ()))
    lane = lax.broadcasted_iota(jnp.int32, (q_ref.shape[0], LANE), 1)
    for p in range(pairs):
        outs = []
        for hh in range(2):
            c0 = (2 * p + hh) * LANE
            q = q_ref[:, c0:c0 + LANE]
            m = l = acc = None
            for i in range(nseg):
                k_ref, v_ref = refs[2 * i], refs[2 * i + 1]
                kc = min(k_ref.shape[0], KEY_CHUNK)
                for c in range(k_ref.shape[0] // kc):
                    rows = slice(c * kc, (c + 1) * kc)
                    s = lax.dot_general(q, k_ref[rows, c0:c0 + LANE], nt_dims, preferred_element_type=F32)
                    v = v_ref[rows, p * LANE:(p + 1) * LANE]
                    ms = jnp.max(s, axis=-1, keepdims=True)
                    if m is None:
                        m = ms
                        e = jnp.exp(s - m)
                        l = jnp.sum(e, axis=-1, keepdims=True)
                        acc = jnp.dot(e.astype(BF16), v, preferred_element_type=F32)
                    else:
                        m_new = jnp.maximum(m, ms)
                        alpha = jnp.exp(m - m_new)
                        e = jnp.exp(s - m_new)
                        l = alpha * l + jnp.sum(e, axis=-1, keepdims=True)
                        acc = alpha * acc + jnp.dot(e.astype(BF16), v, preferred_element_type=F32)
                        m = m_new
            outs.append(acc / l)
        o_ref[:, p * LANE:(p + 1) * LANE] = jnp.where(lane < vh, outs[0], outs[1]).astype(BF16)


def attention(dm, q, segs, row0, nseq, length, tq, pairs, prev=None):
    npg = dm.heads // 2 // pairs
    nqt = length // tq
    qb0 = row0 // tq
    in_specs = [pl.BlockSpec((tq, pairs * 2 * LANE), lambda s, g, t: (qb0 + s * nqt + t, g))]
    args = [q]
    for (k, v, lk, r0) in segs:
        kb0 = r0 // lk
        in_specs.append(pl.BlockSpec((lk, pairs * 2 * LANE), lambda s, g, t, kb0=kb0: (kb0 + s, g)))
        in_specs.append(pl.BlockSpec((lk, pairs * LANE), lambda s, g, t, kb0=kb0: (kb0 + s, g)))
        args += [k, v]
    aliases = {}
    if prev is not None:
        in_specs.append(pl.BlockSpec(memory_space=pl.ANY))
        args.append(prev)
        aliases = {len(args) - 1: 0}
    return pl.pallas_call(
        functools.partial(_attn_kernel, nseg=len(segs), pairs=pairs, vh=dm.vh),
        out_shape=jax.ShapeDtypeStruct((dm.nt, dm.heads * dm.vh), BF16),
        grid=(nseq, npg, nqt),
        in_specs=in_specs,
        out_specs=pl.BlockSpec((tq, pairs * LANE), lambda s, g, t: (qb0 + s * nqt + t, g)),
        input_output_aliases=aliases,
        compiler_params=_cparams("arbitrary", "arbitrary", "arbitrary"),
        name="attention_%d" % length,
    )(*args)


def _conf_kernel(a_ref, g_ref, w_ref, b_ref, lg_ref, lb_ref, *rest, length, ktaps, sub):
    o_ref, hs_ref, sh_ref, w8_ref = rest[-4:]
    cw = a_ref.shape[1]
    off = HALO - (ktaps - 1) // 2
    span = sub + SUBLANE * ((off + ktaps - 1) // SUBLANE)
    groups = CONV_CHUNK // SUBLANE
    hs_ref[0:HALO, :] = jnp.zeros((HALO, cw), F32)
    hs_ref[HALO + length:, :] = jnp.zeros((HALO, cw), F32)
    hs_ref[HALO:HALO + length, :] = a_ref[...].astype(F32) * jax.nn.sigmoid(g_ref[...].astype(F32))
    for j in range(ktaps):
        w8_ref[j] = jnp.broadcast_to(w_ref[j:j + 1, :], (SUBLANE, cw))
    bias, lg, lb = b_ref[...], lg_ref[...], lb_ref[...]

    for sb in range(length // sub):
        base = sb * sub
        for r in range(SUBLANE):
            sh_ref[r] = hs_ref[base + r:base + r + span, :]

        def body(c, carry):
            t0 = pl.multiple_of(c * CONV_CHUNK, CONV_CHUNK)
            acc = jnp.broadcast_to(bias, (groups, SUBLANE, cw))
            for j in range(ktaps):
                q, r = divmod(off + j, SUBLANE)
                xj = sh_ref[r, pl.ds(t0 + SUBLANE * q, CONV_CHUNK), :].reshape(groups, SUBLANE, cw)
                acc = acc + w8_ref[j] * xj
            acc = acc.reshape(CONV_CHUNK, cw)
            mu = jnp.mean(acc, axis=-1, keepdims=True)
            xc = acc - mu
            var = jnp.mean(xc * xc, axis=-1, keepdims=True)
            y = xc * lax.rsqrt(var + EPS) * lg + lb
            o_ref[pl.ds(base + t0, CONV_CHUNK), :] = _silu(y).astype(BF16)
            return carry

        lax.fori_loop(0, sub // CONV_CHUNK, body, 0, unroll=2)


def conformer(dm, p, cf_col0, row0, nseq, length, w, b, lg, lb, prev=None):
    cw = dm.cw
    rb0 = row0 // length
    cb0 = cf_col0 // cw
    one = lambda s: (0, 0)
    in_specs = [pl.BlockSpec((length, cw), lambda s: (rb0 + s, cb0)),
                pl.BlockSpec((length, cw), lambda s: (rb0 + s, cb0 + 1)),
                pl.BlockSpec((dm.cf_k, cw), one), pl.BlockSpec((1, cw), one),
                pl.BlockSpec((1, cw), one), pl.BlockSpec((1, cw), one)]
    args = [p, p, w, b, lg, lb]
    aliases = {}
    if prev is not None:
        in_specs.append(pl.BlockSpec(memory_space=pl.ANY))
        args.append(prev)
        aliases = {len(args) - 1: 0}
    sub = min(length, CONV_SUB)
    span = sub + SUBLANE * ((HALO + (dm.cf_k - 1) // 2) // SUBLANE)
    return pl.pallas_call(
        functools.partial(_conf_kernel, length=length, ktaps=dm.cf_k, sub=sub),
        out_shape=jax.ShapeDtypeStruct((dm.nt, cw), BF16),
        grid=(nseq,),
        in_specs=in_specs,
        out_specs=pl.BlockSpec((length, cw), lambda s: (rb0 + s, 0)),
        scratch_shapes=[pltpu.VMEM((length + 2 * HALO, cw), F32),
                        pltpu.VMEM((SUBLANE, span, cw), F32),
                        pltpu.VMEM((dm.cf_k, SUBLANE, cw), F32)],
        input_output_aliases=aliases,
        compiler_params=_cparams("arbitrary"),
        name="conformer_%d" % length,
    )(*args)


def _merge_kernel(z_ref, o_ref, h_ref, ga_ref, gb_ref, gc_ref, wa_ref, wb_ref, wc_ref, bc_ref, m_ref):
    ya = jnp.dot(z_ref[...], wa_ref[...], preferred_element_type=F32)
    yb = jnp.dot(o_ref[...], wb_ref[...], preferred_element_type=F32)
    yc = jnp.dot(h_ref[...], wc_ref[...], preferred_element_type=F32) + bc_ref[...]
    sg = lambda r: jax.nn.sigmoid(r[...].astype(F32))
    m_ref[...] = (sg(ga_ref) * ya + sg(gb_ref) * yb + sg(gc_ref) * yc).astype(BF16)


def merge(dm, z2, o, hc, p, gate_col0, wa, wb, wc, bc):
    tm, d = dm.tm_s, dm.d
    gb0 = gate_col0 // d
    row = lambda i: (i, 0)
    one = lambda i: (0, 0)
    return pl.pallas_call(
        _merge_kernel,
        out_shape=jax.ShapeDtypeStruct((dm.nt, d), BF16),
        grid=(dm.nt // tm,),
        in_specs=[pl.BlockSpec((tm, dm.hw), row), pl.BlockSpec((tm, dm.heads * dm.vh), row),
                  pl.BlockSpec((tm, dm.cw), row),
                  pl.BlockSpec((tm, d), lambda i: (i, gb0)), pl.BlockSpec((tm, d), lambda i: (i, gb0 + 1)),
                  pl.BlockSpec((tm, d), lambda i: (i, gb0 + 2)),
                  _resident((dm.hw, d), one), _resident((dm.heads * dm.vh, d), one),
                  _resident((dm.cw, d), one), pl.BlockSpec((1, d), one)],
        out_specs=pl.BlockSpec((tm, d), row),
        compiler_params=_cparams("arbitrary"),
        name="merge",
    )(z2, o, hc, p, p, p, wa, wb, wc, bc)


def _outproj_kernel(m_ref, w_ref, x_ref, mod_ref, g_ref, wr_ref, br_ref, x1_ref, h2_ref, lg_ref, *, d):
    y = jnp.dot(m_ref[...], w_ref[...], preferred_element_type=F32)
    g1 = mod_ref[0, :, 2 * d:3 * d]
    sh2 = mod_ref[0, :, 3 * d:4 * d]
    sc2 = mod_ref[0, :, 4 * d:5 * d]
    x1 = x_ref[...] + g1 * y
    x1_ref[...] = x1
    h2 = _rms(x1, g_ref[...]) * (1.0 + sc2) + sh2
    h2_ref[...] = _pack_bf16_pairs(h2)
    lg_ref[...] = jnp.dot(h2.astype(BF16), wr_ref[...], preferred_element_type=F32) + br_ref[...]


def outproj(dm, merged, w_out, x, mod_l, g, wr, br):
    tm, d = dm.tm_s, dm.d
    cid = _cond_id(dm, tm)
    row = lambda i: (i, 0)
    one = lambda i: (0, 0)
    return pl.pallas_call(
        functools.partial(_outproj_kernel, d=d),
        out_shape=(jax.ShapeDtypeStruct((dm.nt, d), F32), jax.ShapeDtypeStruct((dm.nt, d // 2), U32),
                   jax.ShapeDtypeStruct((dm.nt, LANE), F32)),
        grid=(dm.nt // tm,),
        in_specs=[pl.BlockSpec((tm, d), row), _resident((d, d), one), pl.BlockSpec((tm, d), row),
                  pl.BlockSpec((1, 1, 6 * d), lambda i: (cid(i), 0, 0)), pl.BlockSpec((1, d), one),
                  pl.BlockSpec((d, LANE), one), pl.BlockSpec((1, LANE), one)],
        out_specs=(pl.BlockSpec((tm, d), row), pl.BlockSpec((tm, d // 2), row), pl.BlockSpec((tm, LANE), row)),
        compiler_params=_cparams("arbitrary"),
        name="outproj",
    )(merged, w_out, x, mod_l, g, wr, br)


def _route_kernel(lg_ref, info_ref, cnt_ref, carry_ref, *, n_exp, groups, epg):
    @pl.when(pl.program_id(0) == 0)
    def _():
        carry_ref[...] = jnp.zeros_like(carry_ref)

    x = lg_ref[...]
    lane = lax.broadcasted_iota(jnp.int32, x.shape, 1).astype(F32)
    big = jnp.float32(1e9)
    neg = jnp.float32(-jnp.inf)
    is_g = (lane >= n_exp) & (lane < n_exp + groups)
    xg = jnp.where(is_g, x, neg)
    mg = jnp.max(xg, axis=-1, keepdims=True)
    sg = jnp.sum(jnp.where(is_g, jnp.exp(xg - mg), 0.0), axis=-1, keepdims=True)
    pg_top = 1.0 / sg
    gidx = jnp.min(jnp.where(xg == mg, lane, big), axis=-1, keepdims=True) - n_exp
    lo = gidx * epg
    in_grp = (lane >= lo) & (lane < lo + epg)
    xe = jnp.where(in_grp, x, neg)
    m1 = jnp.max(xe, axis=-1, keepdims=True)
    e1 = jnp.min(jnp.where(xe == m1, lane, big), axis=-1, keepdims=True)
    xe2 = jnp.where(lane == e1, neg, xe)
    m2 = jnp.max(xe2, axis=-1, keepdims=True)
    e2 = jnp.min(jnp.where(xe2 == m2, lane, big), axis=-1, keepdims=True)
    t = jnp.exp(m2 - m1)
    w1 = pg_top / (1.0 + t)
    w2 = pg_top * t / (1.0 + t)
    tm = x.shape[0]
    o1 = jnp.where(lane == e1, 1.0, 0.0)
    o2 = jnp.where(lane == e2, 1.0, 0.0)
    rr = lax.broadcasted_iota(jnp.int32, (tm, tm), 0)
    cc = lax.broadcasted_iota(jnp.int32, (tm, tm), 1)
    tri = jnp.where(cc < rr, 1.0, 0.0).astype(BF16)
    cum1 = jnp.dot(tri, o1.astype(BF16), preferred_element_type=F32)
    cum2 = jnp.dot(tri, o2.astype(BF16), preferred_element_type=F32)
    tot1 = jnp.sum(o1, axis=0, keepdims=True)
    tot2 = jnp.sum(o2, axis=0, keepdims=True)
    carry = carry_ref[0:1, :]
    rank1 = jnp.sum(o1 * (carry + cum1), axis=-1, keepdims=True)
    rank2 = jnp.sum(o2 * (carry + tot1 + cum2), axis=-1, keepdims=True)
    new = jnp.broadcast_to(carry + tot1 + tot2, carry_ref.shape)
    carry_ref[...] = new
    cnt_ref[...] = new
    cols = (e1, e2, rank1, rank2, w1, w2)
    info = jnp.zeros_like(x)
    for k, col in enumerate(cols):
        info = jnp.where(lane == k, col, info)
    info_ref[...] = info


def route(dm, logits):
    tm = dm.tm_s
    return pl.pallas_call(
        functools.partial(_route_kernel, n_exp=dm.n_exp, groups=dm.groups, epg=dm.epg),
        out_shape=(jax.ShapeDtypeStruct((dm.nt, LANE), F32), jax.ShapeDtypeStruct((SUBLANE, LANE), F32)),
        grid=(dm.nt // tm,),
        in_specs=[pl.BlockSpec((tm, LANE), lambda i: (i, 0))],
        out_specs=(pl.BlockSpec((tm, LANE), lambda i: (i, 0)), pl.BlockSpec((SUBLANE, LANE), lambda i: (0, 0))),
        scratch_shapes=[pltpu.VMEM((SUBLANE, LANE), F32)],
        compiler_params=_cparams("arbitrary"),
        name="route",
    )(logits)


def _positions_kernel(info_ref, cnt_ref, pos_ref, meta_ref, *, n_exp, tile):
    cnt = cnt_ref[0:1, :]
    padded = jnp.floor((cnt + (tile - 1.0)) * (1.0 / tile)) * tile
    a = lax.broadcasted_iota(jnp.int32, (LANE, LANE), 0)
    b = lax.broadcasted_iota(jnp.int32, (LANE, LANE), 1)
    before = jnp.where(a < b, 1.0, 0.0).astype(BF16)
    off = jnp.dot(jnp.broadcast_to(padded, (SUBLANE, LANE)).astype(BF16), before,
                  preferred_element_type=F32)[0:1, :]
    info = info_ref[...]
    lane = lax.broadcasted_iota(jnp.int32, info.shape, 1).astype(F32)
    e1, e2, r1, r2 = info[:, 0:1], info[:, 1:2], info[:, 2:3], info[:, 3:4]
    p1 = r1 + jnp.sum(jnp.where(lane == e1, off, 0.0), axis=-1, keepdims=True)
    p2 = r2 + jnp.sum(jnp.where(lane == e2, off, 0.0), axis=-1, keepdims=True)
    pos_ref[...] = jnp.where(lane == 0, p1, jnp.where(lane == 1, p2, 0.0)).astype(jnp.int32)

    end = off + padded
    end_col = jnp.sum(jnp.where(a == b, jnp.broadcast_to(end, (LANE, LANE)), 0.0), axis=1, keepdims=True)
    start = lax.broadcasted_iota(jnp.int32, (1, 2 * LANE), 1).astype(F32) * tile
    te = jnp.sum(jnp.where(end_col <= start, 1.0, 0.0), axis=0, keepdims=True)
    te = jnp.minimum(te, n_exp - 1.0)
    n_used = jnp.sum(padded, axis=-1, keepdims=True) * (1.0 / tile)
    tail = jnp.maximum(end - tile, 0.0)
    meta_ref[...] = jnp.zeros_like(meta_ref)
    meta_ref[0:1, :] = te.astype(jnp.int32)
    meta_ref[1:2, 0:LANE] = tail.astype(jnp.int32)
    meta_ref[2:3, :] = jnp.broadcast_to(n_used, (1, 2 * LANE)).astype(jnp.int32)


def positions(dm, info, cnt):
    tm = dm.tm_s
    return pl.pallas_call(
        functools.partial(_positions_kernel, n_exp=dm.n_exp, tile=dm.moe_tile),
        out_shape=(jax.ShapeDtypeStruct((dm.nt, LANE), jnp.int32),
                   jax.ShapeDtypeStruct((SUBLANE, 2 * LANE), jnp.int32)),
        grid=(dm.nt // tm,),
        in_specs=[pl.BlockSpec((tm, LANE), lambda i: (i, 0)), pl.BlockSpec((SUBLANE, LANE), lambda i: (0, 0))],
        out_specs=(pl.BlockSpec((tm, LANE), lambda i: (i, 0)),
                   pl.BlockSpec((SUBLANE, 2 * LANE), lambda i: (0, 0))),
        compiler_params=_cparams("arbitrary"),
        name="positions",
    )(info, cnt)


def _row_copy(src, src_row, dst, dst_row, sem):
    return pltpu.make_async_copy(src.at[pl.ds(src_row, 1)], dst.at[pl.ds(dst_row, 1)], sem)


def _dispatch_kernel(tail_ref, pos1_ref, pos2_ref, h_ref, xs_ref, zero_ref, sem, zsem, *, n_exp, tile, tb):
    i = pl.program_id(0)

    def zero_copy(e):
        start = pl.multiple_of(tail_ref[e], SUBLANE)
        return pltpu.make_async_copy(zero_ref, xs_ref.at[pl.ds(start, tile)], zsem)

    @pl.when(i == 0)
    def _():
        zero_ref[...] = jnp.zeros_like(zero_ref)
        for e in range(n_exp):
            zero_copy(e).start()
        for e in range(n_exp):
            zero_copy(e).wait()

    def body(t, carry):
        _row_copy(h_ref, t, xs_ref, pos1_ref[0, 0, t], sem).start()
        _row_copy(h_ref, t, xs_ref, pos2_ref[0, 0, t], sem).start(priority=1)
        return carry

    lax.fori_loop(0, tb, body, 0, unroll=8)
    pltpu.make_async_copy(xs_ref.at[pl.ds(0, 2 * tb)], xs_ref.at[pl.ds(0, 2 * tb)], sem).wait()


def dispatch(dm, tail, pos1, pos2, h2p):
    tb, tile = dm.tm_s, dm.moe_tile
    rows = 2 * dm.nt + dm.n_exp * tile
    blk = pl.BlockSpec((1, 1, tb), lambda i, tail: (i, 0, 0), memory_space=pltpu.SMEM)
    return pl.pallas_call(
        functools.partial(_dispatch_kernel, n_exp=dm.n_exp, tile=tile, tb=tb),
        out_shape=jax.ShapeDtypeStruct((rows, dm.d // 2), U32),
        grid_spec=pltpu.PrefetchScalarGridSpec(
            num_scalar_prefetch=1, grid=(dm.nt // tb,),
            in_specs=[blk, blk, pl.BlockSpec((tb, dm.d // 2), lambda i, tail: (i, 0))],
            out_specs=pl.BlockSpec(memory_space=pl.ANY),
            scratch_shapes=[pltpu.VMEM((tile, dm.d // 2), U32), pltpu.SemaphoreType.DMA,
                            pltpu.SemaphoreType.DMA]),
        compiler_params=_cparams("arbitrary"),
        name="dispatch",
    )(tail, pos1, pos2, h2p)


def _experts_kernel(te_ref, nu_ref, x_ref, wg_ref, wu_ref, wd_ref, y_ref, wgb_ref, wub_ref, wdb_ref):
    r = pl.program_id(0)

    @pl.when(r < nu_ref[0])
    def _():
        @pl.when((r == 0) | (te_ref[r] != te_ref[jnp.maximum(r - 1, 0)]))
        def _():
            wgb_ref[...] = wg_ref[0, 0].astype(BF16)
            wub_ref[...] = wu_ref[0, 0].astype(BF16)
            wdb_ref[...] = wd_ref[0, 0].astype(BF16)

        x = _unpack_bf16_pairs(x_ref[...]).astype(BF16)
        hg = jnp.dot(x, wgb_ref[...], preferred_element_type=F32)
        hu = jnp.dot(x, wub_ref[...], preferred_element_type=F32)
        a = (_silu(hg) * hu).astype(BF16)
        y_ref[...] = _pack_bf16_pairs(jnp.dot(a, wdb_ref[...], preferred_element_type=F32))


def experts(dm, layer, te, n_used, xs, w_eg, w_eu, w_ed):
    tile, d, fe = dm.moe_tile, dm.d, dm.fe
    rows = xs.shape[0]
    last = lambda r, nu: jnp.minimum(r, nu[0] - 1)
    wmap = lambda r, te, nu: (layer, te[last(r, nu)], 0, 0)
    return pl.pallas_call(
        _experts_kernel,
        out_shape=jax.ShapeDtypeStruct((rows, d // 2), U32),
        grid_spec=pltpu.PrefetchScalarGridSpec(
            num_scalar_prefetch=2, grid=(rows // tile,),
            in_specs=[pl.BlockSpec((tile, d // 2), lambda r, te, nu: (last(r, nu), 0)),
                      pl.BlockSpec((1, 1, d, fe), wmap), pl.BlockSpec((1, 1, d, fe), wmap),
                      pl.BlockSpec((1, 1, fe, d), wmap)],
            out_specs=pl.BlockSpec((tile, d // 2), lambda r, te, nu: (last(r, nu), 0)),
            scratch_shapes=[pltpu.VMEM((d, fe), BF16), pltpu.VMEM((d, fe), BF16), pltpu.VMEM((fe, d), BF16)]),
        compiler_params=_cparams("arbitrary"),
        name="experts",
    )(te, n_used, xs, w_eg, w_eu, w_ed)


def _combine_kernel(pos1_ref, pos2_ref, ys_ref, info_ref, x1_ref, mod_ref, o_ref, b1_ref, b2_ref, sem,
                    *, d, tb):
    def body(t, carry):
        _row_copy(ys_ref, pos1_ref[0, 0, t], b1_ref, t, sem).start()
        _row_copy(ys_ref, pos2_ref[0, 0, t], b2_ref, t, sem).start(priority=1)
        return carry

    lax.fori_loop(0, tb, body, 0, unroll=8)
    pltpu.make_async_copy(ys_ref.at[pl.ds(0, tb)], b1_ref, sem).wait()
    pltpu.make_async_copy(ys_ref.at[pl.ds(0, tb)], b2_ref, sem).wait()
    info = info_ref[...]
    y = info[:, 4:5] * _unpack_bf16_pairs(b1_ref[...]) + info[:, 5:6] * _unpack_bf16_pairs(b2_ref[...])
    o_ref[...] = x1_ref[...] + mod_ref[0, :, 5 * d:6 * d] * y


def combine(dm, pos1, pos2, ys, info, x1, mod_l):
    tb, d = dm.tm_s, dm.d
    cid = _cond_id(dm, tb)
    blk = pl.BlockSpec((1, 1, tb), lambda i: (i, 0, 0), memory_space=pltpu.SMEM)
    row = lambda i: (i, 0)
    return pl.pallas_call(
        functools.partial(_combine_kernel, d=d, tb=tb),
        out_shape=jax.ShapeDtypeStruct((dm.nt, d), F32),
        grid=(dm.nt // tb,),
        in_specs=[blk, blk, pl.BlockSpec(memory_space=pl.ANY), pl.BlockSpec((tb, LANE), row),
                  pl.BlockSpec((tb, d), row), pl.BlockSpec((1, 1, 6 * d), lambda i: (cid(i), 0, 0))],
        out_specs=pl.BlockSpec((tb, d), row),
        scratch_shapes=[pltpu.VMEM((tb, d // 2), U32), pltpu.VMEM((tb, d // 2), U32),
                        pltpu.SemaphoreType.DMA],
        compiler_params=_cparams("arbitrary"),
        name="combine",
    )(pos1, pos2, ys, info, x1, mod_l)


def _dft_kernel(c1_ref, s1_ref, c2_ref, s2_ref, c_ref, s_ref):
    c1, s1, c2, s2 = c1_ref[0], s1_ref[0], c2_ref[...], s2_ref[...]
    c_ref[...] = (c1 * c2 - s1 * s2).astype(BF16)
    s_ref[...] = (s1 * c2 + c1 * s2).astype(BF16)


def _dft_tables(length):
    blk = min(length, 256)
    nblk = length // blk
    s = jnp.arange(length, dtype=jnp.int32)

    def angle(k):
        return ((k[:, None] * s[None, :]) % (2 * length)).astype(F32) * (math.pi / length)

    a1 = angle(jnp.arange(nblk, dtype=jnp.int32) * blk)
    a2 = angle(jnp.arange(blk, dtype=jnp.int32))
    row = pl.BlockSpec((1, 1, length), lambda a: (a, 0, 0))
    full = pl.BlockSpec((blk, length), lambda a: (0, 0))
    out = jax.ShapeDtypeStruct((length, length), BF16)
    return pl.pallas_call(
        _dft_kernel,
        out_shape=(out, out),
        grid=(nblk,),
        in_specs=[row, row, full, full],
        out_specs=(pl.BlockSpec((blk, length), lambda a: (a, 0)), pl.BlockSpec((blk, length), lambda a: (a, 0))),
        compiler_params=_cparams("arbitrary"),
        name="dft_tables_%d" % length,
    )(jnp.cos(a1)[:, None, :], jnp.sin(a1)[:, None, :], jnp.cos(a2), jnp.sin(a2))


def _head_layout(dm, nope, rope):
    half = dm.rope // 2
    a = LANE // 2 - half
    ref = nope if nope is not None else rope
    z = lambda n: jnp.zeros(ref.shape[:-1] + (n,), ref.dtype)
    r1, r2 = (rope[..., :half], rope[..., half:]) if rope is not None else (z(half), z(half))
    n1, n2 = (nope[..., :a], nope[..., a:]) if nope is not None else (z(a), z(dm.nope - a))
    return jnp.concatenate([r1, n1, r2, n2, z(LANE - dm.dk)], axis=-1)


def _rope_tables(dm):
    length = dm.dec_seq
    n_freq = dm.rope // 4
    pos = jnp.arange(length, dtype=jnp.int32)
    row = (pos // dm.grid_w).astype(F32)
    col = (pos % dm.grid_w).astype(F32)
    inv = jnp.power(ROPE_BASE, -jnp.arange(n_freq, dtype=F32) / n_freq)
    ang = jnp.concatenate([row[:, None] * inv[None, :], col[:, None] * inv[None, :]], axis=-1)
    cos, sin = jnp.cos(ang), jnp.sin(ang)
    cos_t = _head_layout(dm, jnp.ones((length, dm.nope), F32), jnp.concatenate([cos, cos], axis=1))
    sin_t = _head_layout(dm, None, jnp.concatenate([-sin, sin], axis=1))
    return cos_t, sin_t


def _pack_w_in(dm, w_in):
    o = 0
    parts = {}
    for name, n in (("hy", 3 * dm.hw), ("cq", dm.q_lora), ("ckv", dm.kv_lora), ("kpe", dm.rope),
                    ("cf", 2 * dm.cw), ("gate", 3 * dm.d)):
        parts[name] = w_in[:, o:o + n]
        o += n
    used = dm.q_lora + dm.kv_lora + LANE
    small = [parts["cq"], parts["ckv"], _head_layout(dm, None, parts["kpe"]),
             jnp.zeros((w_in.shape[0], dm.tn - used), w_in.dtype)]
    return jnp.concatenate(small + [parts["gate"], parts["hy"], parts["cf"]], axis=1).astype(BF16)


def _forward(dm, x_prompt, x_sample, cache_ckv, cache_kpe, c, c_ctx, w_mod, b_mod, norm_mix, norm_ffn,
             w_in, hy_short, hy_f_w1, hy_f_b1, hy_f_w2, hy_f_b2, hy_f_w3, hy_decay, hy_bias, w_hy_out,
             q_a_norm, w_uq, kv_a_norm, w_ukv, q_norm, k_norm, w_mla_out,
             cf_dw, cf_dw_b, cf_ln_g, cf_ln_b, w_cf_out, b_cf_out, w_out,
             w_rg, b_rg, w_re, b_re, w_eg, w_eu, w_ed):
    d, hw, cw, heads = dm.d, dm.hw, dm.cw, dm.heads
    n_ctx, nt = dm.n_ctx, dm.nt
    gate_col0, hy_col0, cf_col0 = 0, 3 * d, 3 * d + 3 * hw
    ckv_blk = dm.q_lora // dm.kv_lora
    kpe_blk = (dm.q_lora + dm.kv_lora) // LANE
    kpe_lane0 = dm.q_lora + dm.kv_lora
    half = dm.rope // 2

    x = jnp.concatenate([x_prompt.reshape(n_ctx, d), x_sample.reshape(dm.n_lat, d)], axis=0)
    cond8 = jnp.concatenate([c_ctx[None, :], c, jnp.zeros((8 - 1 - dm.dec_batch, d), F32)], axis=0)
    mod = adaln(dm, cond8, w_mod, b_mod)

    dft = {dm.seq: _dft_tables(dm.seq), dm.dec_seq: _dft_tables(dm.dec_seq)}
    cos_t, sin_t = _rope_tables(dm)
    scale = float(dm.dk) ** -0.5
    tc_ctx = min(hw, 1024)
    tq_ctx = dm.seq
    pairs_ctx = heads // 2
    tm_cache = min(dm.dec_batch * dm.past, dm.tm_s)

    ckv_out, kpe_out = [], []
    for l in range(dm.depth):
        mod_l = mod[l][:, None, :]
        w_packed = _pack_w_in(dm, w_in[l])
        p2, p = inproj(dm, x, mod_l, norm_mix[l][None, :], w_packed)

        w1p = jnp.zeros((LANE, dm.hy_fhid), F32).at[:1 + 2 * dm.hy_freqs].set(hy_f_w1[l])
        z2 = None
        for (length, row0, nseq, tc, nb) in ((dm.seq, 0, dm.batch, tc_ctx, 1),
                                             (dm.dec_seq, n_ctx, dm.dec_batch, min(hw, dm.tc_hy), dm.hy_nb)):
            cmat, smat = dft[length]
            filt = hyena_filters(dm, length, w1p, hy_f_b1[l][None, :], hy_f_w2[l], hy_f_b2[l][None, :],
                                 hy_f_w3[l], hy_decay[l][None, :])
            kc, ks, kn = hyena_spectra(dm, length, filt, cmat, smat)
            z2 = hyena_conv(dm, p, hy_col0, row0, nseq, length, tc, nb, hy_short[l], hy_bias[l], kc, ks, kn,
                            cmat, smat, prev=z2)

        wq = w_uq[l].reshape(dm.q_lora, heads, dm.dk)
        w_uq_p = _head_layout(dm, wq[..., :dm.nope], wq[..., dm.nope:])
        w_uq_p = w_uq_p.reshape(dm.q_lora, heads * LANE).astype(BF16)
        wkv = w_ukv[l].reshape(dm.kv_lora, heads, dm.nope + dm.vh)
        wk_p = _head_layout(dm, wkv[:, :, :dm.nope], None)
        w_ukv_p = jnp.concatenate([wk_p.reshape(dm.kv_lora, heads * LANE),
                                   wkv[:, :, dm.nope:].reshape(dm.kv_lora, heads * dm.vh)], axis=1).astype(BF16)
        gq = _head_layout(dm, q_norm[l][:dm.nope] * scale, q_norm[l][dm.nope:] * scale)[None, :]
        gk = _head_layout(dm, k_norm[l][:dm.nope], k_norm[l][dm.nope:])[None, :]
        q = mla_q(dm, p2, q_a_norm[l][None, :], w_uq_p, gq, cos_t, sin_t)
        k, v, ckv_n = mla_kv(dm, p2, ckv_blk, p2, kpe_blk, nt, dm.tm_s, kv_a_norm[l][None, :], w_ukv_p, gk,
                             rope=(cos_t, sin_t))
        kpe_c = _head_layout(dm, None, cache_kpe[:, l].reshape(dm.dec_batch * dm.past, dm.rope))
        k_c, v_c, _ = mla_kv(dm, cache_ckv[:, l].reshape(dm.dec_batch * dm.past, dm.kv_lora), 0, kpe_c, 0,
                             dm.dec_batch * dm.past, tm_cache, kv_a_norm[l][None, :], w_ukv_p, gk)
        o = attention(dm, q, [(k, v, dm.seq, 0)], 0, dm.batch, dm.seq, tq_ctx, pairs_ctx)
        o = attention(dm, q, [(k, v, dm.dec_seq, n_ctx), (k_c, v_c, dm.past, 0)], n_ctx, dm.dec_batch,
                      dm.dec_seq, min(dm.tq, dm.dec_seq), 1, prev=o)
        ckv_out.append(ckv_n[:n_ctx].reshape(dm.batch, dm.seq, dm.kv_lora))
        kpe_blk_ctx = p2[:n_ctx, kpe_lane0:kpe_lane0 + LANE]
        kpe_out.append(jnp.concatenate([kpe_blk_ctx[:, :half], kpe_blk_ctx[:, LANE // 2:LANE // 2 + half]],
                                       axis=1).reshape(dm.batch, dm.seq, dm.rope))

        hc = conformer(dm, p, cf_col0, 0, dm.batch, dm.seq, cf_dw[l], cf_dw_b[l][None, :],
                       cf_ln_g[l][None, :], cf_ln_b[l][None, :])
        hc = conformer(dm, p, cf_col0, n_ctx, dm.dec_batch, dm.dec_seq, cf_dw[l], cf_dw_b[l][None, :],
                       cf_ln_g[l][None, :], cf_ln_b[l][None, :], prev=hc)

        merged = merge(dm, z2, o, hc, p, gate_col0, w_hy_out[l].astype(BF16), w_mla_out[l].astype(BF16),
                       w_cf_out[l].astype(BF16), b_cf_out[l][None, :])
        wr = jnp.concatenate([w_re[l], w_rg[l], jnp.zeros((d, LANE - dm.n_exp - dm.groups), F32)],
                             axis=1).astype(BF16)
        br = jnp.concatenate([b_re[l], b_rg[l], jnp.zeros((LANE - dm.n_exp - dm.groups,), F32)])[None, :]
        x1, h2p, logits = outproj(dm, merged, w_out[l].astype(BF16), x, mod_l, norm_ffn[l][None, :], wr, br)

        info, cnt = route(dm, logits)
        pos, meta = positions(dm, info, cnt)
        pos1 = pos[:, 0].reshape(nt // dm.tm_s, 1, dm.tm_s)
        pos2 = pos[:, 1].reshape(nt // dm.tm_s, 1, dm.tm_s)
        n_tiles = (2 * nt + dm.n_exp * dm.moe_tile) // dm.moe_tile
        xs = dispatch(dm, meta[1, :dm.n_exp], pos1, pos2, h2p)
        ys = experts(dm, l, meta[0, :n_tiles], meta[2, :1], xs, w_eg, w_eu, w_ed)
        x = combine(dm, pos1, pos2, ys, info, x1, mod_l)

    y_prompt = x[:n_ctx].reshape(dm.batch, dm.seq, d)
    y_sample = x[n_ctx:].reshape(dm.dec_batch, dm.dec_seq, d)
    return y_prompt, y_sample, jnp.stack(ckv_out, axis=1), jnp.stack(kpe_out, axis=1)


def kernel(x_prompt, x_sample, cache_ckv, cache_kpe, c, c_ctx, w_mod, b_mod, norm_mix, norm_ffn, w_in, hy_short, hy_f_w1, hy_f_b1, hy_f_w2, hy_f_b2, hy_f_w3, hy_decay, hy_bias, w_hy_out, q_a_norm, w_uq, kv_a_norm, w_ukv, q_norm, k_norm, w_mla_out, cf_dw, cf_dw_b, cf_ln_g, cf_ln_b, w_cf_out, b_cf_out, w_out, w_rg, b_rg, w_re, b_re, w_eg, w_eu, w_ed):
    return _forward(Dims(), x_prompt, x_sample, cache_ckv, cache_kpe, c, c_ctx, w_mod, b_mod, norm_mix,
                    norm_ffn, w_in, hy_short, hy_f_w1, hy_f_b1, hy_f_w2, hy_f_b2, hy_f_w3, hy_decay, hy_bias,
                    w_hy_out, q_a_norm, w_uq, kv_a_norm, w_ukv, q_norm, k_norm, w_mla_out,
                    cf_dw, cf_dw_b, cf_ln_g, cf_ln_b, w_cf_out, b_cf_out, w_out,
                    w_rg, b_rg, w_re, b_re, w_eg, w_eu, w_ed)
```

```python
import functools
import math
from typing import NamedTuple

import jax
import jax.numpy as jnp
from jax import lax
from jax.experimental import pallas as pl
from jax.experimental.pallas import tpu as pltpu

F32 = jnp.float32
BF16 = jnp.bfloat16
U32 = jnp.uint32
EPS = 1e-6
LANE = 128
SUBLANE = 8
CONV_SUB = 256
KEY_CHUNK = 512
VMEM_LIMIT_BYTES = 56 << 20
ROPE_BASE = 10000.0
HY_ORDER = 2
HY_SHORT = 3
HALO = 16
CONV_CHUNK = 32
NORM_CHUNK = 16


class Dims(NamedTuple):
    d: int = 2048
    batch: int = 32
    seq: int = 256
    depth: int = 4
    dec_batch: int = 4
    dec_seq: int = 2048
    past: int = 256
    grid_w: int = 64
    hw: int = 1024
    hy_freqs: int = 16
    hy_fhid: int = 64
    heads: int = 16
    nope: int = 64
    rope: int = 32
    vh: int = 64
    q_lora: int = 512
    kv_lora: int = 256
    cw: int = 1024
    cf_k: int = 31
    groups: int = 4
    epg: int = 4
    fe: int = 512
    tm: int = 1024
    tm_s: int = 512
    tn: int = 1024
    tq: int = 1024
    tc_hy: int = 256
    hy_nb: int = 2
    hy_rb: int = 1024
    moe_tile: int = 256

    @property
    def n_ctx(self):
        return self.batch * self.seq

    @property
    def n_lat(self):
        return self.dec_batch * self.dec_seq

    @property
    def nt(self):
        return self.n_ctx + self.n_lat

    @property
    def n_exp(self):
        return self.groups * self.epg

    @property
    def dk(self):
        return self.nope + self.rope


def _cparams(*sem):
    return pltpu.CompilerParams(dimension_semantics=sem, vmem_limit_bytes=VMEM_LIMIT_BYTES)


def _resident(shape, index_map):
    return pl.BlockSpec(shape, index_map, pipeline_mode=pl.Buffered(1))


def _silu(x):
    return x * jax.nn.sigmoid(x)


def _pack_bf16_pairs(x):
    n = x.shape[1] // 2
    lo = lax.bitcast_convert_type(x[:, :n].astype(jnp.bfloat16).astype(F32), U32) >> 16
    hi = lax.bitcast_convert_type(x[:, n:].astype(jnp.bfloat16).astype(F32), U32) & jnp.uint32(0xFFFF0000)
    return hi | lo


def _unpack_bf16_pairs(u):
    lo = lax.bitcast_convert_type(u << 16, F32)
    hi = lax.bitcast_convert_type(u & jnp.uint32(0xFFFF0000), F32)
    return jnp.concatenate([lo, hi], axis=1)


def _rms(x, g):
    return x * lax.rsqrt(jnp.mean(x * x, axis=-1, keepdims=True) + EPS) * g


def _cond_id(dm, tile):
    n_ctx_tiles = dm.n_ctx // tile
    per_seq = dm.dec_seq // tile
    return lambda i: jnp.where(i < n_ctx_tiles, 0, 1 + (i - n_ctx_tiles) // per_seq)


def _rope_blk(dm, tile):
    n_ctx_tiles = dm.n_ctx // tile
    per_seq = dm.dec_seq // tile
    return lambda i: jnp.where(i < n_ctx_tiles, 0, (i - n_ctx_tiles) % per_seq)


def _adaln_kernel(c_ref, w_ref, b_ref, o_ref):
    a = _silu(c_ref[...]).astype(BF16)
    o_ref[0] = jnp.dot(a, w_ref[0].astype(BF16), preferred_element_type=F32) + b_ref[0]


def adaln(dm, cond8, w_mod, b_mod):
    depth, d, n = w_mod.shape
    tn = min(n, 1024)
    return pl.pallas_call(
        _adaln_kernel,
        out_shape=jax.ShapeDtypeStruct((depth, 8, n), F32),
        grid=(depth, n // tn),
        in_specs=[pl.BlockSpec((8, d), lambda l, j: (0, 0)),
                  pl.BlockSpec((1, d, tn), lambda l, j: (l, 0, j)),
                  pl.BlockSpec((1, 1, tn), lambda l, j: (l, 0, j))],
        out_specs=pl.BlockSpec((1, 8, tn), lambda l, j: (l, 0, j)),
        compiler_params=_cparams("arbitrary", "arbitrary"),
        name="adaln",
    )(cond8, w_mod, b_mod.reshape(depth, 1, n))


def _inproj_kernel(x_ref, mod_ref, g_ref, w_ref, p2_ref, p_ref, hn_ref, *, d):
    j = pl.program_id(1)

    @pl.when(j == 0)
    def _():
        y = _rms(x_ref[...], g_ref[...])
        sh = mod_ref[0, :, 0:d]
        sc = mod_ref[0, :, d:2 * d]
        hn_ref[...] = (y * (1.0 + sc) + sh).astype(BF16)

    acc = jnp.dot(hn_ref[...], w_ref[...], preferred_element_type=F32)
    p_ref[...] = acc.astype(BF16)

    @pl.when(j == 0)
    def _():
        p2_ref[...] = acc


def inproj(dm, x, mod_l, g, w_packed):
    nt, d = x.shape
    tm, tn = dm.tm, dm.tn
    ncols = w_packed.shape[1]
    nj = ncols // tn
    cid = _cond_id(dm, tm)
    return pl.pallas_call(
        functools.partial(_inproj_kernel, d=d),
        out_shape=(jax.ShapeDtypeStruct((nt, tn), F32),
                   jax.ShapeDtypeStruct((nt, ncols - tn), BF16)),
        grid=(nt // tm, nj),
        in_specs=[pl.BlockSpec((tm, d), lambda i, j: (i, 0)),
                  pl.BlockSpec((1, 1, 6 * d), lambda i, j: (cid(i), 0, 0)),
                  pl.BlockSpec((1, d), lambda i, j: (0, 0)),
                  pl.BlockSpec((d, tn), lambda i, j: (0, j))],
        out_specs=(pl.BlockSpec((tm, tn), lambda i, j: (i, 0)),
                   pl.BlockSpec((tm, tn), lambda i, j: (i, jnp.maximum(j - 1, 0)))),
        scratch_shapes=[pltpu.VMEM((tm, d), BF16)],
        compiler_params=_cparams("arbitrary", "arbitrary"),
        name="inproj",
    )(x, mod_l, g, w_packed)


def _dot3(a, b):
    ah = a.astype(BF16)
    al = (a - ah.astype(F32)).astype(BF16)
    bh = b.astype(BF16)
    bl = (b - bh.astype(F32)).astype(BF16)
    return (jnp.dot(ah, bh, preferred_element_type=F32) + jnp.dot(al, bh, preferred_element_type=F32)
            + jnp.dot(ah, bl, preferred_element_type=F32))


def _hyfilt_kernel(w1_ref, b1_ref, w2_ref, b2_ref, w3_ref, dec_ref, f_ref, h_ref, *, length, nfreq):
    @pl.when(pl.program_id(0) == 0)
    def _():
        row = lax.broadcasted_iota(jnp.int32, (length, LANE), 0).astype(F32)
        lane = lax.broadcasted_iota(jnp.int32, (length, LANE), 1)
        tn = row / float(length)
        fr = jnp.where(lane <= nfreq, lane, lane - nfreq).astype(F32)
        ang = (2.0 * math.pi) * tn * fr
        feat = jnp.where(lane == 0, tn, jnp.where(lane <= nfreq, jnp.cos(ang),
                                                  jnp.where(lane <= 2 * nfreq, jnp.sin(ang), 0.0)))
        h = jnp.sin(_dot3(feat, w1_ref[...]) + b1_ref[...])
        h_ref[...] = jnp.sin(_dot3(h, w2_ref[...]) + b2_ref[...])

    tn = lax.broadcasted_iota(jnp.int32, (length, 1), 0).astype(F32) / float(length)
    f_ref[...] = _dot3(h_ref[...], w3_ref[...]) * jnp.exp(-tn * dec_ref[...])


def hyena_filters(dm, length, w1p, b1, w2, b2, w3, decay):
    ncol = w3.shape[1]
    tn = min(ncol, 1024)
    fh = w2.shape[0]
    return pl.pallas_call(
        functools.partial(_hyfilt_kernel, length=length, nfreq=dm.hy_freqs),
        out_shape=jax.ShapeDtypeStruct((length, ncol), F32),
        grid=(ncol // tn,),
        in_specs=[pl.BlockSpec((LANE, fh), lambda j: (0, 0)),
                  pl.BlockSpec((1, fh), lambda j: (0, 0)),
                  pl.BlockSpec((fh, fh), lambda j: (0, 0)),
                  pl.BlockSpec((1, fh), lambda j: (0, 0)),
                  pl.BlockSpec((fh, tn), lambda j: (0, j)),
                  pl.BlockSpec((1, tn), lambda j: (0, j))],
        out_specs=pl.BlockSpec((length, tn), lambda j: (0, j)),
        scratch_shapes=[pltpu.VMEM((length, fh), F32)],
        compiler_params=_cparams("arbitrary"),
        name="hyena_filters",
    )(w1p, b1, w2, b2, w3, decay)


def _hyspec_kernel(hf_ref, hb_ref, c_ref, s_ref, kc_ref, ks_ref, kn_ref, *, length):
    row = lax.broadcasted_iota(jnp.int32, (length, 1), 0)
    hf = hf_ref[...]
    hb0 = jnp.where(row == 0, 0.0, hb_ref[...])
    sp = hf + hb0
    sm = hf - hb0
    inv = 1.0 / length
    kc = jnp.dot(c_ref[...], sp.astype(BF16), preferred_element_type=F32)
    kc_ref[0] = kc * jnp.where(row == 0, 0.5 * inv, inv)
    ks_ref[0] = jnp.dot(s_ref[...], sm.astype(BF16), preferred_element_type=F32) * inv
    sgn = (1 - 2 * (row & 1)).astype(F32)
    kn = jnp.sum(sgn * sp, axis=0, keepdims=True) * (0.5 * inv)
    kn_ref[0] = jnp.broadcast_to(kn, kn_ref.shape[1:])


def hyena_spectra(dm, length, filt, cmat, smat):
    hw = dm.hw
    tc = min(hw, 512)
    nct = hw // tc
    shp = jax.ShapeDtypeStruct((HY_ORDER, length, hw), F32)
    return pl.pallas_call(
        functools.partial(_hyspec_kernel, length=length),
        out_shape=(shp, shp, jax.ShapeDtypeStruct((HY_ORDER, 8, hw), F32)),
        grid=(HY_ORDER, nct),
        in_specs=[pl.BlockSpec((length, tc), lambda n, j: (0, (2 * n) * nct + j)),
                  pl.BlockSpec((length, tc), lambda n, j: (0, (2 * n + 1) * nct + j)),
                  _resident((length, length), lambda n, j: (0, 0)),
                  _resident((length, length), lambda n, j: (0, 0))],
        out_specs=(pl.BlockSpec((1, length, tc), lambda n, j: (n, 0, j)),
                   pl.BlockSpec((1, length, tc), lambda n, j: (n, 0, j)),
                   pl.BlockSpec((1, 8, tc), lambda n, j: (n, 0, j))),
        compiler_params=_cparams("arbitrary", "arbitrary"),
        name="hyena_spectra",
    )(filt, filt, cmat, smat)


def _hyconv_kernel(v_ref, x1_ref, x2_ref, swv_ref, sw1_ref, sw2_ref, bias_ref, kc_ref, ks_ref, kn_ref,
                   c_ref, s_ref, *rest, length, nb, rb):
    o_ref, zb_ref, zc_ref, zs_ref, z1_ref = rest[-5:]
    tc = v_ref.shape[1]
    nblk = length // rb
    rowi = lax.broadcasted_iota(jnp.int32, (rb, 1), 0)
    sgn = (1 - 2 * (rowi & 1)).astype(F32)
    halo = 2 * SUBLANE

    def lanes(x):
        return jnp.concatenate([x] * nb, axis=1) if nb > 1 else x

    def short(x_ref, w_ref, q, r):
        base = q * length + r * rb
        x = x_ref[base:base + rb, :].astype(F32)
        prev = (jnp.zeros((1, tc), F32) if r == 0
                else x_ref[base - halo:base, :].astype(F32)[halo - 1:halo, :])
        nxt = (jnp.zeros((1, tc), F32) if r == nblk - 1
               else x_ref[base + rb:base + rb + halo, :].astype(F32)[0:1, :])
        xp = jnp.where(rowi == 0, prev, pltpu.roll(x, 1, 0))
        xn = jnp.where(rowi == rb - 1, nxt, pltpu.roll(x, rb - 1, 0))
        return w_ref[0:1, :] * xp + w_ref[1:2, :] * x + w_ref[2:3, :] * xn

    def forward(n):
        for r in range(nblk):
            rows = slice(r * rb, (r + 1) * rb)
            uc = jnp.dot(c_ref[rows, :], zb_ref[...], preferred_element_type=F32)
            us = jnp.dot(s_ref[rows, :], zb_ref[...], preferred_element_type=F32)
            kc = lanes(kc_ref[n, rows, :])
            ks = lanes(ks_ref[n, rows, :])
            zc_ref[rows, :] = (uc * kc - us * ks).astype(BF16)
            zs_ref[rows, :] = (uc * ks + us * kc).astype(BF16)

    def inverse(n, r, un):
        rows = slice(r * rb, (r + 1) * rb)
        y = jnp.dot(c_ref[rows, :], zc_ref[...], preferred_element_type=F32)
        y = y + jnp.dot(s_ref[rows, :], zs_ref[...], preferred_element_type=F32)
        return y + sgn * (un * lanes(kn_ref[n, 0:1, :]))

    un = []
    for q in range(nb):
        acc = jnp.zeros((1, tc), F32)
        for r in range(nblk):
            zv = short(v_ref, swv_ref, q, r)
            zb_ref[r * rb:(r + 1) * rb, q * tc:(q + 1) * tc] = zv.astype(BF16)
            acc = acc + jnp.sum(sgn * zv, axis=0, keepdims=True)
        un.append(acc)
    un = jnp.concatenate(un, axis=1) if nb > 1 else un[0]
    forward(0)
    un2 = [jnp.zeros((1, tc), F32) for _ in range(nb)]
    for r in range(nblk):
        rows = slice(r * rb, (r + 1) * rb)
        y = inverse(0, r, un)
        for q in range(nb):
            cols = slice(q * tc, (q + 1) * tc)
            zv = short(v_ref, swv_ref, q, r)
            z1 = short(x1_ref, sw1_ref, q, r) * (y[:, cols] + zv * bias_ref[0:1, :])
            z1_ref[rows, cols] = z1
            zb_ref[rows, cols] = z1.astype(BF16)
            un2[q] = un2[q] + jnp.sum(sgn * z1, axis=0, keepdims=True)
    un2 = jnp.concatenate(un2, axis=1) if nb > 1 else un2[0]
    forward(1)
    for r in range(nblk):
        rows = slice(r * rb, (r + 1) * rb)
        y = inverse(1, r, un2)
        for q in range(nb):
            cols = slice(q * tc, (q + 1) * tc)
            z2 = short(x2_ref, sw2_ref, q, r) * (y[:, cols] + z1_ref[rows, cols] * bias_ref[1:2, :])
            o_ref[q * length + r * rb:q * length + (r + 1) * rb, :] = z2.astype(BF16)


def hyena_conv(dm, p, hy_col0, row0, nseq, length, tc, nb, short_w, bias, kc, ks, kn, cmat, smat, prev=None):
    hw = dm.hw
    nct = hw // tc
    rows = nb * length
    rb0 = row0 // rows
    cb0 = hy_col0 // tc
    rb = min(length, dm.hy_rb)
    spec = _resident if rows * tc >= (1 << 20) else pl.BlockSpec

    def part(k):
        return spec((rows, tc), lambda j, s, k=k: (rb0 + s, cb0 + k * nct + j))

    def sw(k):
        return pl.BlockSpec((HY_SHORT, tc), lambda j, s, k=k: (0, k * nct + j))

    in_specs = [part(0), part(1), part(2), sw(0), sw(1), sw(2),
                pl.BlockSpec((HY_ORDER, tc), lambda j, s: (0, j)),
                _resident((HY_ORDER, length, tc), lambda j, s: (0, 0, j)),
                _resident((HY_ORDER, length, tc), lambda j, s: (0, 0, j)),
                pl.BlockSpec((HY_ORDER, 8, tc), lambda j, s: (0, 0, j)),
                _resident((length, length), lambda j, s: (0, 0)),
                _resident((length, length), lambda j, s: (0, 0))]
    args = [p, p, p, short_w, short_w, short_w, bias, kc, ks, kn, cmat, smat]
    aliases = {}
    if prev is not None:
        in_specs.append(pl.BlockSpec(memory_space=pl.ANY))
        args.append(prev)
        aliases = {len(args) - 1: 0}
    return pl.pallas_call(
        functools.partial(_hyconv_kernel, length=length, nb=nb, rb=rb),
        out_shape=jax.ShapeDtypeStruct((dm.nt, hw), BF16),
        grid=(nct, nseq // nb),
        in_specs=in_specs,
        out_specs=pl.BlockSpec((rows, tc), lambda j, s: (rb0 + s, j)),
        scratch_shapes=[pltpu.VMEM((length, nb * tc), BF16), pltpu.VMEM((length, nb * tc), BF16),
                        pltpu.VMEM((length, nb * tc), BF16), pltpu.VMEM((length, nb * tc), F32)],
        input_output_aliases=aliases,
        compiler_params=_cparams("arbitrary", "arbitrary"),
        name="hyena_conv_%d" % length,
    )(*args)


def _store_heads(x, gain, o_ref, rot, *, heads, dk):
    ones = jnp.ones((LANE, LANE), BF16)
    for h in range(heads):
        xh = x[:, h * LANE:(h + 1) * LANE]
        ss = jnp.dot((xh * xh).astype(BF16), ones, preferred_element_type=F32)
        xh = xh * lax.rsqrt(ss * (1.0 / dk) + EPS) * gain
        if rot is not None:
            xh = xh * rot[0] + pltpu.roll(xh, LANE // 2, 1) * rot[1]
        o_ref[:, h * LANE:(h + 1) * LANE] = xh.astype(BF16)


def _store_heads_by_segment(x, gain, o_ref, cos_ref, sin_ref, n_ctx_tiles, **kw):
    is_lat = pl.program_id(0) >= n_ctx_tiles

    @pl.when(is_lat)
    def _():
        _store_heads(x, gain, o_ref, (cos_ref[...], sin_ref[...]), **kw)

    @pl.when(jnp.logical_not(is_lat))
    def _():
        _store_heads(x, gain, o_ref, None, **kw)


def _q_kernel(cq_ref, ga_ref, w_ref, gh_ref, cos_ref, sin_ref, o_ref, *, heads, dk, n_ctx_tiles):
    y = _rms(cq_ref[...], ga_ref[...])
    q = jnp.dot(y.astype(BF16), w_ref[...], preferred_element_type=F32)
    _store_heads_by_segment(q, gh_ref[...], o_ref, cos_ref, sin_ref, n_ctx_tiles, heads=heads, dk=dk)


def mla_q(dm, p2, ga, w_uq, gh, cos_t, sin_t):
    tm = dm.tm_s
    rb = _rope_blk(dm, tm)
    tab = pl.BlockSpec((tm, LANE), lambda i: (rb(i), 0))
    return pl.pallas_call(
        functools.partial(_q_kernel, heads=dm.heads, dk=dm.dk, n_ctx_tiles=dm.n_ctx // tm),
        out_shape=jax.ShapeDtypeStruct((dm.nt, dm.heads * LANE), BF16),
        grid=(dm.nt // tm,),
        in_specs=[pl.BlockSpec((tm, dm.q_lora), lambda i: (i, 0)),
                  pl.BlockSpec((1, dm.q_lora), lambda i: (0, 0)),
                  pl.BlockSpec((dm.q_lora, dm.heads * LANE), lambda i: (0, 0)),
                  pl.BlockSpec((1, LANE), lambda i: (0, 0)),
                  tab, tab],
        out_specs=pl.BlockSpec((tm, dm.heads * LANE), lambda i: (i, 0)),
        compiler_params=_cparams("arbitrary"),
        name="mla_q",
    )(p2, ga, w_uq, gh, cos_t, sin_t)


def _kv_kernel(ckv_ref, kpe_ref, ga_ref, w_ref, gk_ref, *rest, heads, dk, n_ctx_tiles, from_cache):
    k_ref, v_ref, cn_ref = rest[-3:]
    x = ckv_ref[...]
    if not from_cache:
        x = _rms(x, ga_ref[...])
    cn_ref[...] = x
    kv = jnp.dot(x.astype(BF16), w_ref[...], preferred_element_type=F32)
    kpe = kpe_ref[...]
    k = kv[:, :heads * LANE] + jnp.concatenate([kpe] * heads, axis=1)
    if from_cache:
        _store_heads(k, gk_ref[...], k_ref, None, heads=heads, dk=dk)
    else:
        _store_heads_by_segment(k, gk_ref[...], k_ref, rest[0], rest[1], n_ctx_tiles, heads=heads, dk=dk)
    v_ref[...] = kv[:, heads * LANE:].astype(BF16)


def mla_kv(dm, ckv_src, ckv_blk, kpe_src, kpe_blk, nrows, tm, ga, w_ukv, gk, rope=None):
    hk = dm.heads * LANE
    hv = dm.heads * dm.vh
    in_specs = [pl.BlockSpec((tm, dm.kv_lora), lambda i: (i, ckv_blk)),
                pl.BlockSpec((tm, LANE), lambda i: (i, kpe_blk)),
                pl.BlockSpec((1, dm.kv_lora), lambda i: (0, 0)),
                pl.BlockSpec((dm.kv_lora, hk + hv), lambda i: (0, 0)),
                pl.BlockSpec((1, LANE), lambda i: (0, 0))]
    args = [ckv_src, kpe_src, ga, w_ukv, gk]
    if rope is not None:
        rb = _rope_blk(dm, tm)
        in_specs += [pl.BlockSpec((tm, LANE), lambda i: (rb(i), 0))] * 2
        args += list(rope)
    return pl.pallas_call(
        functools.partial(_kv_kernel, heads=dm.heads, dk=dm.dk, n_ctx_tiles=dm.n_ctx // tm,
                          from_cache=rope is None),
        out_shape=(jax.ShapeDtypeStruct((nrows, hk), BF16),
                   jax.ShapeDtypeStruct((nrows, hv), BF16),
                   jax.ShapeDtypeStruct((nrows, dm.kv_lora), F32)),
        grid=(nrows // tm,),
        in_specs=in_specs,
        out_specs=(pl.BlockSpec((tm, hk), lambda i: (i, 0)),
                   pl.BlockSpec((tm, hv), lambda i: (i, 0)),
                   pl.BlockSpec((tm, dm.kv_lora), lambda i: (i, 0))),
        compiler_params=_cparams("arbitrary"),
        name="mla_kv_trunk" if rope is not None else "mla_kv_cache",
    )(*args)


def _attn_kernel(q_ref, *refs, nseg, pairs, vh):
    o_ref = refs[-1]
    nt_dims = (((1,), (1,)), ((), ()))
    lane = lax.broadcasted_iota(jnp.int32, (q_ref.shape[0], LANE), 1)
    for p in range(pairs):
        outs = []
        for hh in range(2):
            c0 = (2 * p + hh) * LANE
            q = q_ref[:, c0:c0 + LANE]
            m = l = acc = None
            for i in range(nseg):
                k_ref, v_ref = refs[2 * i], refs[2 * i + 1]
                kc = min(k_ref.shape[0], KEY_CHUNK)
                for c in range(k_ref.shape[0] // kc):
                    rows = slice(c * kc, (c + 1) * kc)
                    s = lax.dot_general(q, k_ref[rows, c0:c0 + LANE], nt_dims, preferred_element_type=F32)
                    v = v_ref[rows, p * LANE:(p + 1) * LANE]
                    ms = jnp.max(s, axis=-1, keepdims=True)
                    if m is None:
                        m = ms
                        e = jnp.exp(s - m)
                        l = jnp.sum(e, axis=-1, keepdims=True)
                        acc = jnp.dot(e.astype(BF16), v, preferred_element_type=F32)
                    else:
                        m_new = jnp.maximum(m, ms)
                        alpha = jnp.exp(m - m_new)
                        e = jnp.exp(s - m_new)
                        l = alpha * l + jnp.sum(e, axis=-1, keepdims=True)
                        acc = alpha * acc + jnp.dot(e.astype(BF16), v, preferred_element_type=F32)
                        m = m_new
            outs.append(acc / l)
        o_ref[:, p * LANE:(p + 1) * LANE] = jnp.where(lane < vh, outs[0], outs[1]).astype(BF16)


def attention(dm, q, segs, row0, nseq, length, tq, pairs, prev=None):
    npg = dm.heads // 2 // pairs
    nqt = length // tq
    qb0 = row0 // tq
    in_specs = [pl.BlockSpec((tq, pairs * 2 * LANE), lambda s, g, t: (qb0 + s * nqt + t, g))]
    args = [q]
    for (k, v, lk, r0) in segs:
        kb0 = r0 // lk
        in_specs.append(pl.BlockSpec((lk, pairs * 2 * LANE), lambda s, g, t, kb0=kb0: (kb0 + s, g)))
        in_specs.append(pl.BlockSpec((lk, pairs * LANE), lambda s, g, t, kb0=kb0: (kb0 + s, g)))
        args += [k, v]
    aliases = {}
    if prev is not None:
        in_specs.append(pl.BlockSpec(memory_space=pl.ANY))
        args.append(prev)
        aliases = {len(args) - 1: 0}
    return pl.pallas_call(
        functools.partial(_attn_kernel, nseg=len(segs), pairs=pairs, vh=dm.vh),
        out_shape=jax.ShapeDtypeStruct((dm.nt, dm.heads * dm.vh), BF16),
        grid=(nseq, npg, nqt),
        in_specs=in_specs,
        out_specs=pl.BlockSpec((tq, pairs * LANE), lambda s, g, t: (qb0 + s * nqt + t, g)),
        input_output_aliases=aliases,
        compiler_params=_cparams("arbitrary", "arbitrary", "arbitrary"),
        name="attention_%d" % length,
    )(*args)


def _conf_kernel(a_ref, g_ref, w_ref, b_ref, lg_ref, lb_ref, *rest, length, ktaps, sub):
    o_ref, hs_ref, sh_ref, w8_ref, pre_ref = rest[-5:]
    cw = a_ref.shape[1]
    off = HALO - (ktaps - 1) // 2
    span = sub + SUBLANE * ((off + ktaps - 1) // SUBLANE)
    groups = CONV_CHUNK // SUBLANE
    hs_ref[0:HALO, :] = jnp.zeros((HALO, cw), F32)
    hs_ref[HALO + length:, :] = jnp.zeros((HALO, cw), F32)
    hs_ref[HALO:HALO + length, :] = a_ref[...].astype(F32) * jax.nn.sigmoid(g_ref[...].astype(F32))
    for j in range(ktaps):
        w8_ref[j] = jnp.broadcast_to(w_ref[j:j + 1, :], (SUBLANE, cw))
    bias, lg, lb = b_ref[...], lg_ref[...], lb_ref[...]

    for sb in range(length // sub):
        base = sb * sub
        for r in range(SUBLANE):
            sh_ref[r] = hs_ref[base + r:base + r + span, :]

        def conv_body(c, carry):
            t0 = pl.multiple_of(c * CONV_CHUNK, CONV_CHUNK)
            acc = jnp.broadcast_to(bias, (groups, SUBLANE, cw))
            for j in range(ktaps):
                q, r = divmod(off + j, SUBLANE)
                xj = sh_ref[r, pl.ds(t0 + SUBLANE * q, CONV_CHUNK), :].reshape(groups, SUBLANE, cw)
                acc = acc + w8_ref[j] * xj
            pre_ref[pl.ds(t0, CONV_CHUNK), :] = acc.reshape(CONV_CHUNK, cw)
            return carry

        def norm_body(c, carry):
            t0 = pl.multiple_of(c * NORM_CHUNK, NORM_CHUNK)
            acc = pre_ref[pl.ds(t0, NORM_CHUNK), :]
            mu = jnp.mean(acc, axis=-1, keepdims=True)
            xc = acc - mu
            var = jnp.mean(xc * xc, axis=-1, keepdims=True)
            y = xc * lax.rsqrt(var + EPS) * lg + lb
            o_ref[pl.ds(base + t0, NORM_CHUNK), :] = _silu(y).astype(BF16)
            return carry

        lax.fori_loop(0, sub // CONV_CHUNK, conv_body, 0)
        lax.fori_loop(0, sub // NORM_CHUNK, norm_body, 0, unroll=4)


def conformer(dm, p, cf_col0, row0, nseq, length, w, b, lg, lb, prev=None):
    cw = dm.cw
    rb0 = row0 // length
    cb0 = cf_col0 // cw
    one = lambda s: (0, 0)
    in_specs = [pl.BlockSpec((length, cw), lambda s: (rb0 + s, cb0)),
                pl.BlockSpec((length, cw), lambda s: (rb0 + s, cb0 + 1)),
                pl.BlockSpec((dm.cf_k, cw), one), pl.BlockSpec((1, cw), one),
                pl.BlockSpec((1, cw), one), pl.BlockSpec((1, cw), one)]
    args = [p, p, w, b, lg, lb]
    aliases = {}
    if prev is not None:
        in_specs.append(pl.BlockSpec(memory_space=pl.ANY))
        args.append(prev)
        aliases = {len(args) - 1: 0}
    sub = min(length, CONV_SUB)
    span = sub + SUBLANE * ((HALO + (dm.cf_k - 1) // 2) // SUBLANE)
    return pl.pallas_call(
        functools.partial(_conf_kernel, length=length, ktaps=dm.cf_k, sub=sub),
        out_shape=jax.ShapeDtypeStruct((dm.nt, cw), BF16),
        grid=(nseq,),
        in_specs=in_specs,
        out_specs=pl.BlockSpec((length, cw), lambda s: (rb0 + s, 0)),
        scratch_shapes=[pltpu.VMEM((length + 2 * HALO, cw), F32),
                        pltpu.VMEM((SUBLANE, span, cw), F32),
                        pltpu.VMEM((dm.cf_k, SUBLANE, cw), F32),
                        pltpu.VMEM((sub, cw), F32)],
        input_output_aliases=aliases,
        compiler_params=_cparams("arbitrary"),
        name="conformer_%d" % length,
    )(*args)


def _merge_kernel(z_ref, o_ref, h_ref, ga_ref, gb_ref, gc_ref, wa_ref, wb_ref, wc_ref, bc_ref, m_ref):
    ya = jnp.dot(z_ref[...], wa_ref[...], preferred_element_type=F32)
    yb = jnp.dot(o_ref[...], wb_ref[...], preferred_element_type=F32)
    yc = jnp.dot(h_ref[...], wc_ref[...], preferred_element_type=F32) + bc_ref[...]
    sg = lambda r: jax.nn.sigmoid(r[...].astype(F32))
    m_ref[...] = (sg(ga_ref) * ya + sg(gb_ref) * yb + sg(gc_ref) * yc).astype(BF16)


def merge(dm, z2, o, hc, p, gate_col0, wa, wb, wc, bc):
    tm, d = dm.tm_s, dm.d
    gb0 = gate_col0 // d
    row = lambda i: (i, 0)
    one = lambda i: (0, 0)
    return pl.pallas_call(
        _merge_kernel,
        out_shape=jax.ShapeDtypeStruct((dm.nt, d), BF16),
        grid=(dm.nt // tm,),
        in_specs=[pl.BlockSpec((tm, dm.hw), row), pl.BlockSpec((tm, dm.heads * dm.vh), row),
                  pl.BlockSpec((tm, dm.cw), row),
                  pl.BlockSpec((tm, d), lambda i: (i, gb0)), pl.BlockSpec((tm, d), lambda i: (i, gb0 + 1)),
                  pl.BlockSpec((tm, d), lambda i: (i, gb0 + 2)),
                  _resident((dm.hw, d), one), _resident((dm.heads * dm.vh, d), one),
                  _resident((dm.cw, d), one), pl.BlockSpec((1, d), one)],
        out_specs=pl.BlockSpec((tm, d), row),
        compiler_params=_cparams("arbitrary"),
        name="merge",
    )(z2, o, hc, p, p, p, wa, wb, wc, bc)


def _outproj_kernel(m_ref, w_ref, x_ref, mod_ref, g_ref, wr_ref, br_ref, x1_ref, h2_ref, lg_ref, *, d):
    y = jnp.dot(m_ref[...], w_ref[...], preferred_element_type=F32)
    g1 = mod_ref[0, :, 2 * d:3 * d]
    sh2 = mod_ref[0, :, 3 * d:4 * d]
    sc2 = mod_ref[0, :, 4 * d:5 * d]
    x1 = x_ref[...] + g1 * y
    x1_ref[...] = x1
    h2 = _rms(x1, g_ref[...]) * (1.0 + sc2) + sh2
    h2_ref[...] = _pack_bf16_pairs(h2)
    lg_ref[...] = jnp.dot(h2.astype(BF16), wr_ref[...], preferred_element_type=F32) + br_ref[...]


def outproj(dm, merged, w_out, x, mod_l, g, wr, br):
    tm, d = dm.tm_s, dm.d
    cid = _cond_id(dm, tm)
    row = lambda i: (i, 0)
    one = lambda i: (0, 0)
    return pl.pallas_call(
        functools.partial(_outproj_kernel, d=d),
        out_shape=(jax.ShapeDtypeStruct((dm.nt, d), F32), jax.ShapeDtypeStruct((dm.nt, d // 2), U32),
                   jax.ShapeDtypeStruct((dm.nt, LANE), F32)),
        grid=(dm.nt // tm,),
        in_specs=[pl.BlockSpec((tm, d), row), _resident((d, d), one), pl.BlockSpec((tm, d), row),
                  pl.BlockSpec((1, 1, 6 * d), lambda i: (cid(i), 0, 0)), pl.BlockSpec((1, d), one),
                  pl.BlockSpec((d, LANE), one), pl.BlockSpec((1, LANE), one)],
        out_specs=(pl.BlockSpec((tm, d), row), pl.BlockSpec((tm, d // 2), row), pl.BlockSpec((tm, LANE), row)),
        compiler_params=_cparams("arbitrary"),
        name="outproj",
    )(merged, w_out, x, mod_l, g, wr, br)


def _route_kernel(lg_ref, info_ref, cnt_ref, carry_ref, *, n_exp, groups, epg):
    @pl.when(pl.program_id(0) == 0)
    def _():
        carry_ref[...] = jnp.zeros_like(carry_ref)

    x = lg_ref[...]
    lane = lax.broadcasted_iota(jnp.int32, x.shape, 1).astype(F32)
    big = jnp.float32(1e9)
    neg = jnp.float32(-jnp.inf)
    is_g = (lane >= n_exp) & (lane < n_exp + groups)
    xg = jnp.where(is_g, x, neg)
    mg = jnp.max(xg, axis=-1, keepdims=True)
    sg = jnp.sum(jnp.where(is_g, jnp.exp(xg - mg), 0.0), axis=-1, keepdims=True)
    pg_top = 1.0 / sg
    gidx = jnp.min(jnp.where(xg == mg, lane, big), axis=-1, keepdims=True) - n_exp
    lo = gidx * epg
    in_grp = (lane >= lo) & (lane < lo + epg)
    xe = jnp.where(in_grp, x, neg)
    m1 = jnp.max(xe, axis=-1, keepdims=True)
    e1 = jnp.min(jnp.where(xe == m1, lane, big), axis=-1, keepdims=True)
    xe2 = jnp.where(lane == e1, neg, xe)
    m2 = jnp.max(xe2, axis=-1, keepdims=True)
    e2 = jnp.min(jnp.where(xe2 == m2, lane, big), axis=-1, keepdims=True)
    t = jnp.exp(m2 - m1)
    w1 = pg_top / (1.0 + t)
    w2 = pg_top * t / (1.0 + t)
    tm = x.shape[0]
    o1 = jnp.where(lane == e1, 1.0, 0.0)
    o2 = jnp.where(lane == e2, 1.0, 0.0)
    rr = lax.broadcasted_iota(jnp.int32, (tm, tm), 0)
    cc = lax.broadcasted_iota(jnp.int32, (tm, tm), 1)
    tri = jnp.where(cc < rr, 1.0, 0.0).astype(BF16)
    cum1 = jnp.dot(tri, o1.astype(BF16), preferred_element_type=F32)
    cum2 = jnp.dot(tri, o2.astype(BF16), preferred_element_type=F32)
    tot1 = jnp.sum(o1, axis=0, keepdims=True)
    tot2 = jnp.sum(o2, axis=0, keepdims=True)
    carry = carry_ref[0:1, :]
    rank1 = jnp.sum(o1 * (carry + cum1), axis=-1, keepdims=True)
    rank2 = jnp.sum(o2 * (carry + tot1 + cum2), axis=-1, keepdims=True)
    new = jnp.broadcast_to(carry + tot1 + tot2, carry_ref.shape)
    carry_ref[...] = new
    cnt_ref[...] = new
    cols = (e1, e2, rank1, rank2, w1, w2)
    info = jnp.zeros_like(x)
    for k, col in enumerate(cols):
        info = jnp.where(lane == k, col, info)
    info_ref[...] = info


def route(dm, logits):
    tm = dm.tm_s
    return pl.pallas_call(
        functools.partial(_route_kernel, n_exp=dm.n_exp, groups=dm.groups, epg=dm.epg),
        out_shape=(jax.ShapeDtypeStruct((dm.nt, LANE), F32), jax.ShapeDtypeStruct((SUBLANE, LANE), F32)),
        grid=(dm.nt // tm,),
        in_specs=[pl.BlockSpec((tm, LANE), lambda i: (i, 0))],
        out_specs=(pl.BlockSpec((tm, LANE), lambda i: (i, 0)), pl.BlockSpec((SUBLANE, LANE), lambda i: (0, 0))),
        scratch_shapes=[pltpu.VMEM((SUBLANE, LANE), F32)],
        compiler_params=_cparams("arbitrary"),
        name="route",
    )(logits)


def _positions_kernel(info_ref, cnt_ref, pos_ref, meta_ref, *, n_exp, tile):
    cnt = cnt_ref[0:1, :]
    padded = jnp.floor((cnt + (tile - 1.0)) * (1.0 / tile)) * tile
    a = lax.broadcasted_iota(jnp.int32, (LANE, LANE), 0)
    b = lax.broadcasted_iota(jnp.int32, (LANE, LANE), 1)
    before = jnp.where(a < b, 1.0, 0.0).astype(BF16)
    off = jnp.dot(jnp.broadcast_to(padded, (SUBLANE, LANE)).astype(BF16), before,
                  preferred_element_type=F32)[0:1, :]
    info = info_ref[...]
    lane = lax.broadcasted_iota(jnp.int32, info.shape, 1).astype(F32)
    e1, e2, r1, r2 = info[:, 0:1], info[:, 1:2], info[:, 2:3], info[:, 3:4]
    p1 = r1 + jnp.sum(jnp.where(lane == e1, off, 0.0), axis=-1, keepdims=True)
    p2 = r2 + jnp.sum(jnp.where(lane == e2, off, 0.0), axis=-1, keepdims=True)
    pos_ref[...] = jnp.where(lane == 0, p1, jnp.where(lane == 1, p2, 0.0)).astype(jnp.int32)

    end = off + padded
    end_col = jnp.sum(jnp.where(a == b, jnp.broadcast_to(end, (LANE, LANE)), 0.0), axis=1, keepdims=True)
    start = lax.broadcasted_iota(jnp.int32, (1, 2 * LANE), 1).astype(F32) * tile
    te = jnp.sum(jnp.where(end_col <= start, 1.0, 0.0), axis=0, keepdims=True)
    te = jnp.minimum(te, n_exp - 1.0)
    n_used = jnp.sum(padded, axis=-1, keepdims=True) * (1.0 / tile)
    tail = jnp.maximum(end - tile, 0.0)
    meta_ref[...] = jnp.zeros_like(meta_ref)
    meta_ref[0:1, :] = te.astype(jnp.int32)
    meta_ref[1:2, 0:LANE] = tail.astype(jnp.int32)
    meta_ref[2:3, :] = jnp.broadcast_to(n_used, (1, 2 * LANE)).astype(jnp.int32)


def positions(dm, info, cnt):
    tm = dm.tm_s
    return pl.pallas_call(
        functools.partial(_positions_kernel, n_exp=dm.n_exp, tile=dm.moe_tile),
        out_shape=(jax.ShapeDtypeStruct((dm.nt, LANE), jnp.int32),
                   jax.ShapeDtypeStruct((SUBLANE, 2 * LANE), jnp.int32)),
        grid=(dm.nt // tm,),
        in_specs=[pl.BlockSpec((tm, LANE), lambda i: (i, 0)), pl.BlockSpec((SUBLANE, LANE), lambda i: (0, 0))],
        out_specs=(pl.BlockSpec((tm, LANE), lambda i: (i, 0)),
                   pl.BlockSpec((SUBLANE, 2 * LANE), lambda i: (0, 0))),
        compiler_params=_cparams("arbitrary"),
        name="positions",
    )(info, cnt)


def _row_copy(src, src_row, dst, dst_row, sem):
    return pltpu.make_async_copy(src.at[pl.ds(src_row, 1)], dst.at[pl.ds(dst_row, 1)], sem)


def _dispatch_kernel(tail_ref, pos1_ref, pos2_ref, h_ref, xs_ref, zero_ref, sem, zsem, *, n_exp, tile, tb):
    i = pl.program_id(0)

    def zero_copy(e):
        start = pl.multiple_of(tail_ref[e], SUBLANE)
        return pltpu.make_async_copy(zero_ref, xs_ref.at[pl.ds(start, tile)], zsem)

    @pl.when(i == 0)
    def _():
        zero_ref[...] = jnp.zeros_like(zero_ref)
        for e in range(n_exp):
            zero_copy(e).start()
        for e in range(n_exp):
            zero_copy(e).wait()

    def body(t, carry):
        _row_copy(h_ref, t, xs_ref, pos1_ref[0, 0, t], sem).start()
        _row_copy(h_ref, t, xs_ref, pos2_ref[0, 0, t], sem).start(priority=1)
        return carry

    lax.fori_loop(0, tb, body, 0, unroll=8)
    pltpu.make_async_copy(xs_ref.at[pl.ds(0, 2 * tb)], xs_ref.at[pl.ds(0, 2 * tb)], sem).wait()


def dispatch(dm, tail, pos1, pos2, h2p):
    tb, tile = dm.tm_s, dm.moe_tile
    rows = 2 * dm.nt + dm.n_exp * tile
    blk = pl.BlockSpec((1, 1, tb), lambda i, tail: (i, 0, 0), memory_space=pltpu.SMEM)
    return pl.pallas_call(
        functools.partial(_dispatch_kernel, n_exp=dm.n_exp, tile=tile, tb=tb),
        out_shape=jax.ShapeDtypeStruct((rows, dm.d // 2), U32),
        grid_spec=pltpu.PrefetchScalarGridSpec(
            num_scalar_prefetch=1, grid=(dm.nt // tb,),
            in_specs=[blk, blk, pl.BlockSpec((tb, dm.d // 2), lambda i, tail: (i, 0))],
            out_specs=pl.BlockSpec(memory_space=pl.ANY),
            scratch_shapes=[pltpu.VMEM((tile, dm.d // 2), U32), pltpu.SemaphoreType.DMA,
                            pltpu.SemaphoreType.DMA]),
        compiler_params=_cparams("arbitrary"),
        name="dispatch",
    )(tail, pos1, pos2, h2p)


def _experts_kernel(te_ref, nu_ref, x_ref, wg_ref, wu_ref, wd_ref, y_ref, wgb_ref, wub_ref, wdb_ref):
    r = pl.program_id(0)

    @pl.when(r < nu_ref[0])
    def _():
        @pl.when((r == 0) | (te_ref[r] != te_ref[jnp.maximum(r - 1, 0)]))
        def _():
            wgb_ref[...] = wg_ref[0, 0].astype(BF16)
            wub_ref[...] = wu_ref[0, 0].astype(BF16)
            wdb_ref[...] = wd_ref[0, 0].astype(BF16)

        x = _unpack_bf16_pairs(x_ref[...]).astype(BF16)
        hg = jnp.dot(x, wgb_ref[...], preferred_element_type=F32)
        hu = jnp.dot(x, wub_ref[...], preferred_element_type=F32)
        a = (_silu(hg) * hu).astype(BF16)
        y_ref[...] = _pack_bf16_pairs(jnp.dot(a, wdb_ref[...], preferred_element_type=F32))


def experts(dm, layer, te, n_used, xs, w_eg, w_eu, w_ed):
    tile, d, fe = dm.moe_tile, dm.d, dm.fe
    rows = xs.shape[0]
    last = lambda r, nu: jnp.minimum(r, nu[0] - 1)
    wmap = lambda r, te, nu: (layer, te[last(r, nu)], 0, 0)
    return pl.pallas_call(
        _experts_kernel,
        out_shape=jax.ShapeDtypeStruct((rows, d // 2), U32),
        grid_spec=pltpu.PrefetchScalarGridSpec(
            num_scalar_prefetch=2, grid=(rows // tile,),
            in_specs=[pl.BlockSpec((tile, d // 2), lambda r, te, nu: (last(r, nu), 0)),
                      pl.BlockSpec((1, 1, d, fe), wmap), pl.BlockSpec((1, 1, d, fe), wmap),
                      pl.BlockSpec((1, 1, fe, d), wmap)],
            out_specs=pl.BlockSpec((tile, d // 2), lambda r, te, nu: (last(r, nu), 0)),
            scratch_shapes=[pltpu.VMEM((d, fe), BF16), pltpu.VMEM((d, fe), BF16), pltpu.VMEM((fe, d), BF16)]),
        compiler_params=_cparams("arbitrary"),
        name="experts",
    )(te, n_used, xs, w_eg, w_eu, w_ed)


def _combine_kernel(pos1_ref, pos2_ref, ys_ref, info_ref, x1_ref, mod_ref, o_ref, b1_ref, b2_ref, sem,
                    *, d, tb):
    def body(t, carry):
        _row_copy(ys_ref, pos1_ref[0, 0, t], b1_ref, t, sem).start()
        _row_copy(ys_ref, pos2_ref[0, 0, t], b2_ref, t, sem).start(priority=1)
        return carry

    lax.fori_loop(0, tb, body, 0, unroll=8)
    pltpu.make_async_copy(ys_ref.at[pl.ds(0, tb)], b1_ref, sem).wait()
    pltpu.make_async_copy(ys_ref.at[pl.ds(0, tb)], b2_ref, sem).wait()
    info = info_ref[...]
    y = info[:, 4:5] * _unpack_bf16_pairs(b1_ref[...]) + info[:, 5:6] * _unpack_bf16_pairs(b2_ref[...])
    o_ref[...] = x1_ref[...] + mod_ref[0, :, 5 * d:6 * d] * y


def combine(dm, pos1, pos2, ys, info, x1, mod_l):
    tb, d = dm.tm_s, dm.d
    cid = _cond_id(dm, tb)
    blk = pl.BlockSpec((1, 1, tb), lambda i: (i, 0, 0), memory_space=pltpu.SMEM)
    row = lambda i: (i, 0)
    return pl.pallas_call(
        functools.partial(_combine_kernel, d=d, tb=tb),
        out_shape=jax.ShapeDtypeStruct((dm.nt, d), F32),
        grid=(dm.nt // tb,),
        in_specs=[blk, blk, pl.BlockSpec(memory_space=pl.ANY), pl.BlockSpec((tb, LANE), row),
                  pl.BlockSpec((tb, d), row), pl.BlockSpec((1, 1, 6 * d), lambda i: (cid(i), 0, 0))],
        out_specs=pl.BlockSpec((tb, d), row),
        scratch_shapes=[pltpu.VMEM((tb, d // 2), U32), pltpu.VMEM((tb, d // 2), U32),
                        pltpu.SemaphoreType.DMA],
        compiler_params=_cparams("arbitrary"),
        name="combine",
    )(pos1, pos2, ys, info, x1, mod_l)


def _dft_kernel(c1_ref, s1_ref, c2_ref, s2_ref, c_ref, s_ref):
    c1, s1, c2, s2 = c1_ref[0], s1_ref[0], c2_ref[...], s2_ref[...]
    c_ref[...] = (c1 * c2 - s1 * s2).astype(BF16)
    s_ref[...] = (s1 * c2 + c1 * s2).astype(BF16)


def _dft_tables(length):
    blk = min(length, 256)
    nblk = length // blk
    s = jnp.arange(length, dtype=jnp.int32)

    def angle(k):
        return ((k[:, None] * s[None, :]) % (2 * length)).astype(F32) * (math.pi / length)

    a1 = angle(jnp.arange(nblk, dtype=jnp.int32) * blk)
    a2 = angle(jnp.arange(blk, dtype=jnp.int32))
    row = pl.BlockSpec((1, 1, length), lambda a: (a, 0, 0))
    full = pl.BlockSpec((blk, length), lambda a: (0, 0))
    out = jax.ShapeDtypeStruct((length, length), BF16)
    return pl.pallas_call(
        _dft_kernel,
        out_shape=(out, out),
        grid=(nblk,),
        in_specs=[row, row, full, full],
        out_specs=(pl.BlockSpec((blk, length), lambda a: (a, 0)), pl.BlockSpec((blk, length), lambda a: (a, 0))),
        compiler_params=_cparams("arbitrary"),
        name="dft_tables_%d" % length,
    )(jnp.cos(a1)[:, None, :], jnp.sin(a1)[:, None, :], jnp.cos(a2), jnp.sin(a2))


def _head_layout(dm, nope, rope):
    half = dm.rope // 2
    a = LANE // 2 - half
    ref = nope if nope is not None else rope
    z = lambda n: jnp.zeros(ref.shape[:-1] + (n,), ref.dtype)
    r1, r2 = (rope[..., :half], rope[..., half:]) if rope is not None else (z(half), z(half))
    n1, n2 = (nope[..., :a], nope[..., a:]) if nope is not None else (z(a), z(dm.nope - a))
    return jnp.concatenate([r1, n1, r2, n2, z(LANE - dm.dk)], axis=-1)


def _rope_tables(dm):
    length = dm.dec_seq
    n_freq = dm.rope // 4
    pos = jnp.arange(length, dtype=jnp.int32)
    row = (pos // dm.grid_w).astype(F32)
    col = (pos % dm.grid_w).astype(F32)
    inv = jnp.power(ROPE_BASE, -jnp.arange(n_freq, dtype=F32) / n_freq)
    ang = jnp.concatenate([row[:, None] * inv[None, :], col[:, None] * inv[None, :]], axis=-1)
    cos, sin = jnp.cos(ang), jnp.sin(ang)
    cos_t = _head_layout(dm, jnp.ones((length, dm.nope), F32), jnp.concatenate([cos, cos], axis=1))
    sin_t = _head_layout(dm, None, jnp.concatenate([-sin, sin], axis=1))
    return cos_t, sin_t


def _pack_w_in(dm, w_in):
    o = 0
    parts = {}
    for name, n in (("hy", 3 * dm.hw), ("cq", dm.q_lora), ("ckv", dm.kv_lora), ("kpe", dm.rope),
                    ("cf", 2 * dm.cw), ("gate", 3 * dm.d)):
        parts[name] = w_in[:, o:o + n]
        o += n
    used = dm.q_lora + dm.kv_lora + LANE
    small = [parts["cq"], parts["ckv"], _head_layout(dm, None, parts["kpe"]),
             jnp.zeros((w_in.shape[0], dm.tn - used), w_in.dtype)]
    return jnp.concatenate(small + [parts["gate"], parts["hy"], parts["cf"]], axis=1).astype(BF16)


def _forward(dm, x_prompt, x_sample, cache_ckv, cache_kpe, c, c_ctx, w_mod, b_mod, norm_mix, norm_ffn,
             w_in, hy_short, hy_f_w1, hy_f_b1, hy_f_w2, hy_f_b2, hy_f_w3, hy_decay, hy_bias, w_hy_out,
             q_a_norm, w_uq, kv_a_norm, w_ukv, q_norm, k_norm, w_mla_out,
             cf_dw, cf_dw_b, cf_ln_g, cf_ln_b, w_cf_out, b_cf_out, w_out,
             w_rg, b_rg, w_re, b_re, w_eg, w_eu, w_ed):
    d, hw, cw, heads = dm.d, dm.hw, dm.cw, dm.heads
    n_ctx, nt = dm.n_ctx, dm.nt
    gate_col0, hy_col0, cf_col0 = 0, 3 * d, 3 * d + 3 * hw
    ckv_blk = dm.q_lora // dm.kv_lora
    kpe_blk = (dm.q_lora + dm.kv_lora) // LANE
    kpe_lane0 = dm.q_lora + dm.kv_lora
    half = dm.rope // 2

    x = jnp.concatenate([x_prompt.reshape(n_ctx, d), x_sample.reshape(dm.n_lat, d)], axis=0)
    cond8 = jnp.concatenate([c_ctx[None, :], c, jnp.zeros((8 - 1 - dm.dec_batch, d), F32)], axis=0)
    mod = adaln(dm, cond8, w_mod, b_mod)

    dft = {dm.seq: _dft_tables(dm.seq), dm.dec_seq: _dft_tables(dm.dec_seq)}
    cos_t, sin_t = _rope_tables(dm)
    scale = float(dm.dk) ** -0.5
    tc_ctx = min(hw, 1024)
    tq_ctx = dm.seq
    pairs_ctx = heads // 2
    tm_cache = min(dm.dec_batch * dm.past, dm.tm_s)

    ckv_out, kpe_out = [], []
    for l in range(dm.depth):
        mod_l = mod[l][:, None, :]
        w_packed = _pack_w_in(dm, w_in[l])
        p2, p = inproj(dm, x, mod_l, norm_mix[l][None, :], w_packed)

        w1p = jnp.zeros((LANE, dm.hy_fhid), F32).at[:1 + 2 * dm.hy_freqs].set(hy_f_w1[l])
        z2 = None
        for (length, row0, nseq, tc, nb) in ((dm.seq, 0, dm.batch, tc_ctx, 1),
                                             (dm.dec_seq, n_ctx, dm.dec_batch, min(hw, dm.tc_hy), dm.hy_nb)):
            cmat, smat = dft[length]
            filt = hyena_filters(dm, length, w1p, hy_f_b1[l][None, :], hy_f_w2[l], hy_f_b2[l][None, :],
                                 hy_f_w3[l], hy_decay[l][None, :])
            kc, ks, kn = hyena_spectra(dm, length, filt, cmat, smat)
            z2 = hyena_conv(dm, p, hy_col0, row0, nseq, length, tc, nb, hy_short[l], hy_bias[l], kc, ks, kn,
                            cmat, smat, prev=z2)

        wq = w_uq[l].reshape(dm.q_lora, heads, dm.dk)
        w_uq_p = _head_layout(dm, wq[..., :dm.nope], wq[..., dm.nope:])
        w_uq_p = w_uq_p.reshape(dm.q_lora, heads * LANE).astype(BF16)
        wkv = w_ukv[l].reshape(dm.kv_lora, heads, dm.nope + dm.vh)
        wk_p = _head_layout(dm, wkv[:, :, :dm.nope], None)
        w_ukv_p = jnp.concatenate([wk_p.reshape(dm.kv_lora, heads * LANE),
                                   wkv[:, :, dm.nope:].reshape(dm.kv_lora, heads * dm.vh)], axis=1).astype(BF16)
        gq = _head_layout(dm, q_norm[l][:dm.nope] * scale, q_norm[l][dm.nope:] * scale)[None, :]
        gk = _head_layout(dm, k_norm[l][:dm.nope], k_norm[l][dm.nope:])[None, :]
        q = mla_q(dm, p2, q_a_norm[l][None, :], w_uq_p, gq, cos_t, sin_t)
        k, v, ckv_n = mla_kv(dm, p2, ckv_blk, p2, kpe_blk, nt, dm.tm_s, kv_a_norm[l][None, :], w_ukv_p, gk,
                             rope=(cos_t, sin_t))
        kpe_c = _head_layout(dm, None, cache_kpe[:, l].reshape(dm.dec_batch * dm.past, dm.rope))
        k_c, v_c, _ = mla_kv(dm, cache_ckv[:, l].reshape(dm.dec_batch * dm.past, dm.kv_lora), 0, kpe_c, 0,
                             dm.dec_batch * dm.past, tm_cache, kv_a_norm[l][None, :], w_ukv_p, gk)
        o = attention(dm, q, [(k, v, dm.seq, 0)], 0, dm.batch, dm.seq, tq_ctx, pairs_ctx)
        o = attention(dm, q, [(k, v, dm.dec_seq, n_ctx), (k_c, v_c, dm.past, 0)], n_ctx, dm.dec_batch,
                      dm.dec_seq, min(dm.tq, dm.dec_seq), 1, prev=o)
        ckv_out.append(ckv_n[:n_ctx].reshape(dm.batch, dm.seq, dm.kv_lora))
        kpe_blk_ctx = p2[:n_ctx, kpe_lane0:kpe_lane0 + LANE]
        kpe_out.append(jnp.concatenate([kpe_blk_ctx[:, :half], kpe_blk_ctx[:, LANE // 2:LANE // 2 + half]],
                                       axis=1).reshape(dm.batch, dm.seq, dm.rope))

        hc = conformer(dm, p, cf_col0, 0, dm.batch, dm.seq, cf_dw[l], cf_dw_b[l][None, :],
                       cf_ln_g[l][None, :], cf_ln_b[l][None, :])
        hc = conformer(dm, p, cf_col0, n_ctx, dm.dec_batch, dm.dec_seq, cf_dw[l], cf_dw_b[l][None, :],
                       cf_ln_g[l][None, :], cf_ln_b[l][None, :], prev=hc)

        merged = merge(dm, z2, o, hc, p, gate_col0, w_hy_out[l].astype(BF16), w_mla_out[l].astype(BF16),
                       w_cf_out[l].astype(BF16), b_cf_out[l][None, :])
        wr = jnp.concatenate([w_re[l], w_rg[l], jnp.zeros((d, LANE - dm.n_exp - dm.groups), F32)],
                             axis=1).astype(BF16)
        br = jnp.concatenate([b_re[l], b_rg[l], jnp.zeros((LANE - dm.n_exp - dm.groups,), F32)])[None, :]
        x1, h2p, logits = outproj(dm, merged, w_out[l].astype(BF16), x, mod_l, norm_ffn[l][None, :], wr, br)

        info, cnt = route(dm, logits)
        pos, meta = positions(dm, info, cnt)
        pos1 = pos[:, 0].reshape(nt // dm.tm_s, 1, dm.tm_s)
        pos2 = pos[:, 1].reshape(nt // dm.tm_s, 1, dm.tm_s)
        n_tiles = (2 * nt + dm.n_exp * dm.moe_tile) // dm.moe_tile
        xs = dispatch(dm, meta[1, :dm.n_exp], pos1, pos2, h2p)
        ys = experts(dm, l, meta[0, :n_tiles], meta[2, :1], xs, w_eg, w_eu, w_ed)
        x = combine(dm, pos1, pos2, ys, info, x1, mod_l)

    y_prompt = x[:n_ctx].reshape(dm.batch, dm.seq, d)
    y_sample = x[n_ctx:].reshape(dm.dec_batch, dm.dec_seq, d)
    return y_prompt, y_sample, jnp.stack(ckv_out, axis=1), jnp.stack(kpe_out, axis=1)


def kernel(x_prompt, x_sample, cache_ckv, cache_kpe, c, c_ctx, w_mod, b_mod, norm_mix, norm_ffn, w_in, hy_short, hy_f_w1, hy_f_b1, hy_f_w2, hy_f_b2, hy_f_w3, hy_decay, hy_bias, w_hy_out, q_a_norm, w_uq, kv_a_norm, w_ukv, q_norm, k_norm, w_mla_out, cf_dw, cf_dw_b, cf_ln_g, cf_ln_b, w_cf_out, b_cf_out, w_out, w_rg, b_rg, w_re, b_re, w_eg, w_eu, w_ed):
    return _forward(Dims(), x_prompt, x_sample, cache_ckv, cache_kpe, c, c_ctx, w_mod, b_mod, norm_mix,
                    norm_ffn, w_in, hy_short, hy_f_w1, hy_f_b1, hy_f_w2, hy_f_b2, hy_f_w3, hy_decay, hy_bias,
                    w_hy_out, q_a_norm, w_uq, kv_a_norm, w_ukv, q_norm, k_norm, w_mla_out,
                    cf_dw, cf_dw_b, cf_ln_g, cf_ln_b, w_cf_out, b_cf_out, w_out,
                    w_rg, b_rg, w_re, b_re, w_eg, w_eu, w_ed)
```

```python
import functools
import math
from typing import NamedTuple

import jax
import jax.numpy as jnp
from jax import lax
from jax.experimental import pallas as pl
from jax.experimental.pallas import tpu as pltpu

F32 = jnp.float32
BF16 = jnp.bfloat16
U32 = jnp.uint32
EPS = 1e-6
LANE = 128
SUBLANE = 8
CONV_SUB = 256
KEY_CHUNK = 512
VMEM_LIMIT_BYTES = 56 << 20
ROPE_BASE = 10000.0
HY_ORDER = 2
HY_SHORT = 3
HALO = 16
CONV_CHUNK = 32
NORM_CHUNK = 16


class Dims(NamedTuple):
    d: int = 2048
    batch: int = 32
    seq: int = 256
    depth: int = 4
    dec_batch: int = 4
    dec_seq: int = 2048
    past: int = 256
    grid_w: int = 64
    hw: int = 1024
    hy_freqs: int = 16
    hy_fhid: int = 64
    heads: int = 16
    nope: int = 64
    rope: int = 32
    vh: int = 64
    q_lora: int = 512
    kv_lora: int = 256
    cw: int = 1024
    cf_k: int = 31
    groups: int = 4
    epg: int = 4
    fe: int = 512
    tm: int = 1024
    tm_s: int = 512
    tn: int = 1024
    tq: int = 1024
    tc_hy: int = 256
    hy_nb: int = 2
    hy_rb: int = 1024
    moe_tile: int = 256

    @property
    def n_ctx(self):
        return self.batch * self.seq

    @property
    def n_lat(self):
        return self.dec_batch * self.dec_seq

    @property
    def nt(self):
        return self.n_ctx + self.n_lat

    @property
    def n_exp(self):
        return self.groups * self.epg

    @property
    def dk(self):
        return self.nope + self.rope


def _cparams(*sem):
    return pltpu.CompilerParams(dimension_semantics=sem, vmem_limit_bytes=VMEM_LIMIT_BYTES)


def _resident(shape, index_map):
    return pl.BlockSpec(shape, index_map, pipeline_mode=pl.Buffered(1))


def _silu(x):
    return x * jax.nn.sigmoid(x)


def _pack_bf16_pairs(x):
    n = x.shape[1] // 2
    lo = lax.bitcast_convert_type(x[:, :n].astype(jnp.bfloat16).astype(F32), U32) >> 16
    hi = lax.bitcast_convert_type(x[:, n:].astype(jnp.bfloat16).astype(F32), U32) & jnp.uint32(0xFFFF0000)
    return hi | lo


def _unpack_bf16_pairs(u):
    lo = lax.bitcast_convert_type(u << 16, F32)
    hi = lax.bitcast_convert_type(u & jnp.uint32(0xFFFF0000), F32)
    return jnp.concatenate([lo, hi], axis=1)


def _rms(x, g):
    return x * lax.rsqrt(jnp.mean(x * x, axis=-1, keepdims=True) + EPS) * g


def _cond_id(dm, tile):
    n_ctx_tiles = dm.n_ctx // tile
    per_seq = dm.dec_seq // tile
    return lambda i: jnp.where(i < n_ctx_tiles, 0, 1 + (i - n_ctx_tiles) // per_seq)


def _rope_blk(dm, tile):
    n_ctx_tiles = dm.n_ctx // tile
    per_seq = dm.dec_seq // tile
    return lambda i: jnp.where(i < n_ctx_tiles, 0, (i - n_ctx_tiles) % per_seq)


def _adaln_kernel(c_ref, w_ref, b_ref, o_ref):
    a = _silu(c_ref[...]).astype(BF16)
    o_ref[0] = jnp.dot(a, w_ref[0].astype(BF16), preferred_element_type=F32) + b_ref[0]


def adaln(dm, cond8, w_mod, b_mod):
    depth, d, n = w_mod.shape
    tn = min(n, 1024)
    return pl.pallas_call(
        _adaln_kernel,
        out_shape=jax.ShapeDtypeStruct((depth, 8, n), F32),
        grid=(depth, n // tn),
        in_specs=[pl.BlockSpec((8, d), lambda l, j: (0, 0)),
                  pl.BlockSpec((1, d, tn), lambda l, j: (l, 0, j)),
                  pl.BlockSpec((1, 1, tn), lambda l, j: (l, 0, j))],
        out_specs=pl.BlockSpec((1, 8, tn), lambda l, j: (l, 0, j)),
        compiler_params=_cparams("arbitrary", "arbitrary"),
        name="adaln",
    )(cond8, w_mod, b_mod.reshape(depth, 1, n))


def _token_rows(x_refs, n_ctx_tiles):
    if len(x_refs) == 1:
        return x_refs[0][...]
    return jnp.where(pl.program_id(0) < n_ctx_tiles, x_refs[0][...], x_refs[1][...])


def _token_specs(dm, tile, xs):
    n_ctx_tiles = dm.n_ctx // tile
    d = xs[0].shape[1]
    if len(xs) == 1:
        return [pl.BlockSpec((tile, d), lambda i, *_: (i, 0))]
    return [pl.BlockSpec((tile, d), lambda i, *_: (jnp.minimum(i, n_ctx_tiles - 1), 0)),
            pl.BlockSpec((tile, d), lambda i, *_: (jnp.maximum(i - n_ctx_tiles, 0), 0))]


def _inproj_kernel(*refs, d, n_x, n_ctx_tiles):
    mod_ref, g_ref, w_ref, p2_ref, p_ref, hn_ref = refs[n_x:]
    j = pl.program_id(1)

    @pl.when(j == 0)
    def _():
        y = _rms(_token_rows(refs[:n_x], n_ctx_tiles), g_ref[...])
        sh = mod_ref[0, :, 0:d]
        sc = mod_ref[0, :, d:2 * d]
        hn_ref[...] = (y * (1.0 + sc) + sh).astype(BF16)

    acc = jnp.dot(hn_ref[...], w_ref[...], preferred_element_type=F32)
    p_ref[...] = acc.astype(BF16)

    @pl.when(j == 0)
    def _():
        p2_ref[...] = acc


def inproj(dm, xs, mod_l, g, w_packed):
    nt, d = dm.nt, dm.d
    tm, tn = dm.tm, dm.tn
    ncols = w_packed.shape[1]
    nj = ncols // tn
    cid = _cond_id(dm, tm)
    return pl.pallas_call(
        functools.partial(_inproj_kernel, d=d, n_x=len(xs), n_ctx_tiles=dm.n_ctx // tm),
        out_shape=(jax.ShapeDtypeStruct((nt, tn), F32),
                   jax.ShapeDtypeStruct((nt, ncols - tn), BF16)),
        grid=(nt // tm, nj),
        in_specs=_token_specs(dm, tm, xs) + [
                  pl.BlockSpec((1, 1, 6 * d), lambda i, j: (cid(i), 0, 0)),
                  pl.BlockSpec((1, d), lambda i, j: (0, 0)),
                  pl.BlockSpec((d, tn), lambda i, j: (0, j))],
        out_specs=(pl.BlockSpec((tm, tn), lambda i, j: (i, 0)),
                   pl.BlockSpec((tm, tn), lambda i, j: (i, jnp.maximum(j - 1, 0)))),
        scratch_shapes=[pltpu.VMEM((tm, d), BF16)],
        compiler_params=_cparams("arbitrary", "arbitrary"),
        name="inproj",
    )(*xs, mod_l, g, w_packed)


def _dot3(a, b):
    ah = a.astype(BF16)
    al = (a - ah.astype(F32)).astype(BF16)
    bh = b.astype(BF16)
    bl = (b - bh.astype(F32)).astype(BF16)
    return (jnp.dot(ah, bh, preferred_element_type=F32) + jnp.dot(al, bh, preferred_element_type=F32)
            + jnp.dot(ah, bl, preferred_element_type=F32))


def _hyfilt_kernel(w1_ref, b1_ref, w2_ref, b2_ref, w3_ref, dec_ref, f_ref, h_ref, *, length, nfreq):
    @pl.when(pl.program_id(0) == 0)
    def _():
        row = lax.broadcasted_iota(jnp.int32, (length, LANE), 0).astype(F32)
        lane = lax.broadcasted_iota(jnp.int32, (length, LANE), 1)
        tn = row / float(length)
        fr = jnp.where(lane <= nfreq, lane, lane - nfreq).astype(F32)
        ang = (2.0 * math.pi) * tn * fr
        feat = jnp.where(lane == 0, tn, jnp.where(lane <= nfreq, jnp.cos(ang),
                                                  jnp.where(lane <= 2 * nfreq, jnp.sin(ang), 0.0)))
        h = jnp.sin(_dot3(feat, w1_ref[...]) + b1_ref[...])
        h_ref[...] = jnp.sin(_dot3(h, w2_ref[...]) + b2_ref[...])

    tn = lax.broadcasted_iota(jnp.int32, (length, 1), 0).astype(F32) / float(length)
    f_ref[...] = _dot3(h_ref[...], w3_ref[...]) * jnp.exp(-tn * dec_ref[...])


def hyena_filters(dm, length, w1p, b1, w2, b2, w3, decay):
    ncol = w3.shape[1]
    tn = min(ncol, 1024)
    fh = w2.shape[0]
    return pl.pallas_call(
        functools.partial(_hyfilt_kernel, length=length, nfreq=dm.hy_freqs),
        out_shape=jax.ShapeDtypeStruct((length, ncol), F32),
        grid=(ncol // tn,),
        in_specs=[pl.BlockSpec((LANE, fh), lambda j: (0, 0)),
                  pl.BlockSpec((1, fh), lambda j: (0, 0)),
                  pl.BlockSpec((fh, fh), lambda j: (0, 0)),
                  pl.BlockSpec((1, fh), lambda j: (0, 0)),
                  pl.BlockSpec((fh, tn), lambda j: (0, j)),
                  pl.BlockSpec((1, tn), lambda j: (0, j))],
        out_specs=pl.BlockSpec((length, tn), lambda j: (0, j)),
        scratch_shapes=[pltpu.VMEM((length, fh), F32)],
        compiler_params=_cparams("arbitrary"),
        name="hyena_filters",
    )(w1p, b1, w2, b2, w3, decay)


def _hyspec_kernel(hf_ref, hb_ref, c_ref, s_ref, kc_ref, ks_ref, kn_ref, *, length):
    row = lax.broadcasted_iota(jnp.int32, (length, 1), 0)
    hf = hf_ref[...]
    hb0 = jnp.where(row == 0, 0.0, hb_ref[...])
    sp = hf + hb0
    sm = hf - hb0
    inv = 1.0 / length
    kc = jnp.dot(c_ref[...], sp.astype(BF16), preferred_element_type=F32)
    kc_ref[0] = kc * jnp.where(row == 0, 0.5 * inv, inv)
    ks_ref[0] = jnp.dot(s_ref[...], sm.astype(BF16), preferred_element_type=F32) * inv
    sgn = (1 - 2 * (row & 1)).astype(F32)
    kn = jnp.sum(sgn * sp, axis=0, keepdims=True) * (0.5 * inv)
    kn_ref[0] = jnp.broadcast_to(kn, kn_ref.shape[1:])


def hyena_spectra(dm, length, filt, cmat, smat):
    hw = dm.hw
    tc = min(hw, 512)
    nct = hw // tc
    shp = jax.ShapeDtypeStruct((HY_ORDER, length, hw), F32)
    return pl.pallas_call(
        functools.partial(_hyspec_kernel, length=length),
        out_shape=(shp, shp, jax.ShapeDtypeStruct((HY_ORDER, 8, hw), F32)),
        grid=(HY_ORDER, nct),
        in_specs=[pl.BlockSpec((length, tc), lambda n, j: (0, (2 * n) * nct + j)),
                  pl.BlockSpec((length, tc), lambda n, j: (0, (2 * n + 1) * nct + j)),
                  _resident((length, length), lambda n, j: (0, 0)),
                  _resident((length, length), lambda n, j: (0, 0))],
        out_specs=(pl.BlockSpec((1, length, tc), lambda n, j: (n, 0, j)),
                   pl.BlockSpec((1, length, tc), lambda n, j: (n, 0, j)),
                   pl.BlockSpec((1, 8, tc), lambda n, j: (n, 0, j))),
        compiler_params=_cparams("arbitrary", "arbitrary"),
        name="hyena_spectra",
    )(filt, filt, cmat, smat)


def _hyconv_kernel(v_ref, x1_ref, x2_ref, swv_ref, sw1_ref, sw2_ref, bias_ref, kc_ref, ks_ref, kn_ref,
                   c_ref, s_ref, *rest, length, nb, rb):
    o_ref, zb_ref, zc_ref, zs_ref, z1_ref = rest[-5:]
    tc = v_ref.shape[1]
    nblk = length // rb
    rowi = lax.broadcasted_iota(jnp.int32, (rb, 1), 0)
    sgn = (1 - 2 * (rowi & 1)).astype(F32)
    halo = 2 * SUBLANE

    def lanes(x):
        return jnp.concatenate([x] * nb, axis=1) if nb > 1 else x

    def short(x_ref, w_ref, q, r):
        base = q * length + r * rb
        x = x_ref[base:base + rb, :].astype(F32)
        prev = (jnp.zeros((1, tc), F32) if r == 0
                else x_ref[base - halo:base, :].astype(F32)[halo - 1:halo, :])
        nxt = (jnp.zeros((1, tc), F32) if r == nblk - 1
               else x_ref[base + rb:base + rb + halo, :].astype(F32)[0:1, :])
        xp = jnp.where(rowi == 0, prev, pltpu.roll(x, 1, 0))
        xn = jnp.where(rowi == rb - 1, nxt, pltpu.roll(x, rb - 1, 0))
        return w_ref[0:1, :] * xp + w_ref[1:2, :] * x + w_ref[2:3, :] * xn

    def forward(n):
        for r in range(nblk):
            rows = slice(r * rb, (r + 1) * rb)
            uc = jnp.dot(c_ref[rows, :], zb_ref[...], preferred_element_type=F32)
            us = jnp.dot(s_ref[rows, :], zb_ref[...], preferred_element_type=F32)
            kc = lanes(kc_ref[n, rows, :])
            ks = lanes(ks_ref[n, rows, :])
            zc_ref[rows, :] = (uc * kc - us * ks).astype(BF16)
            zs_ref[rows, :] = (uc * ks + us * kc).astype(BF16)

    def inverse(n, r, un):
        rows = slice(r * rb, (r + 1) * rb)
        y = jnp.dot(c_ref[rows, :], zc_ref[...], preferred_element_type=F32)
        y = y + jnp.dot(s_ref[rows, :], zs_ref[...], preferred_element_type=F32)
        return y + sgn * (un * lanes(kn_ref[n, 0:1, :]))

    un = []
    for q in range(nb):
        acc = jnp.zeros((1, tc), F32)
        for r in range(nblk):
            zv = short(v_ref, swv_ref, q, r)
            zb_ref[r * rb:(r + 1) * rb, q * tc:(q + 1) * tc] = zv.astype(BF16)
            acc = acc + jnp.sum(sgn * zv, axis=0, keepdims=True)
        un.append(acc)
    un = jnp.concatenate(un, axis=1) if nb > 1 else un[0]
    forward(0)
    un2 = [jnp.zeros((1, tc), F32) for _ in range(nb)]
    for r in range(nblk):
        rows = slice(r * rb, (r + 1) * rb)
        y = inverse(0, r, un)
        for q in range(nb):
            cols = slice(q * tc, (q + 1) * tc)
            zv = short(v_ref, swv_ref, q, r)
            z1 = short(x1_ref, sw1_ref, q, r) * (y[:, cols] + zv * bias_ref[0:1, :])
            z1_ref[rows, cols] = z1
            zb_ref[rows, cols] = z1.astype(BF16)
            un2[q] = un2[q] + jnp.sum(sgn * z1, axis=0, keepdims=True)
    un2 = jnp.concatenate(un2, axis=1) if nb > 1 else un2[0]
    forward(1)
    for r in range(nblk):
        rows = slice(r * rb, (r + 1) * rb)
        y = inverse(1, r, un2)
        for q in range(nb):
            cols = slice(q * tc, (q + 1) * tc)
            z2 = short(x2_ref, sw2_ref, q, r) * (y[:, cols] + z1_ref[rows, cols] * bias_ref[1:2, :])
            o_ref[q * length + r * rb:q * length + (r + 1) * rb, :] = z2.astype(BF16)


def hyena_conv(dm, p, hy_col0, row0, nseq, length, tc, nb, short_w, bias, kc, ks, kn, cmat, smat, prev=None):
    hw = dm.hw
    nct = hw // tc
    rows = nb * length
    rb0 = row0 // rows
    cb0 = hy_col0 // tc
    rb = min(length, dm.hy_rb)
    spec = _resident if rows * tc >= (1 << 20) else pl.BlockSpec

    def part(k):
        return spec((rows, tc), lambda j, s, k=k: (rb0 + s, cb0 + k * nct + j))

    def sw(k):
        return pl.BlockSpec((HY_SHORT, tc), lambda j, s, k=k: (0, k * nct + j))

    in_specs = [part(0), part(1), part(2), sw(0), sw(1), sw(2),
                pl.BlockSpec((HY_ORDER, tc), lambda j, s: (0, j)),
                _resident((HY_ORDER, length, tc), lambda j, s: (0, 0, j)),
                _resident((HY_ORDER, length, tc), lambda j, s: (0, 0, j)),
                pl.BlockSpec((HY_ORDER, 8, tc), lambda j, s: (0, 0, j)),
                _resident((length, length), lambda j, s: (0, 0)),
                _resident((length, length), lambda j, s: (0, 0))]
    args = [p, p, p, short_w, short_w, short_w, bias, kc, ks, kn, cmat, smat]
    aliases = {}
    if prev is not None:
        in_specs.append(pl.BlockSpec(memory_space=pl.ANY))
        args.append(prev)
        aliases = {len(args) - 1: 0}
    return pl.pallas_call(
        functools.partial(_hyconv_kernel, length=length, nb=nb, rb=rb),
        out_shape=jax.ShapeDtypeStruct((dm.nt, hw), BF16),
        grid=(nct, nseq // nb),
        in_specs=in_specs,
        out_specs=pl.BlockSpec((rows, tc), lambda j, s: (rb0 + s, j)),
        scratch_shapes=[pltpu.VMEM((length, nb * tc), BF16), pltpu.VMEM((length, nb * tc), BF16),
                        pltpu.VMEM((length, nb * tc), BF16), pltpu.VMEM((length, nb * tc), F32)],
        input_output_aliases=aliases,
        compiler_params=_cparams("arbitrary", "arbitrary"),
        name="hyena_conv_%d" % length,
    )(*args)


def _store_heads(x, gain, o_ref, rot, *, heads, dk):
    ones = jnp.ones((LANE, LANE), BF16)
    for h in range(heads):
        xh = x[:, h * LANE:(h + 1) * LANE]
        ss = jnp.dot((xh * xh).astype(BF16), ones, preferred_element_type=F32)
        xh = xh * lax.rsqrt(ss * (1.0 / dk) + EPS) * gain
        if rot is not None:
            xh = xh * rot[0] + pltpu.roll(xh, LANE // 2, 1) * rot[1]
        o_ref[:, h * LANE:(h + 1) * LANE] = xh.astype(BF16)


def _store_heads_by_segment(x, gain, o_ref, cos_ref, sin_ref, n_ctx_tiles, **kw):
    is_lat = pl.program_id(0) >= n_ctx_tiles

    @pl.when(is_lat)
    def _():
        _store_heads(x, gain, o_ref, (cos_ref[...], sin_ref[...]), **kw)

    @pl.when(jnp.logical_not(is_lat))
    def _():
        _store_heads(x, gain, o_ref, None, **kw)


def _q_kernel(cq_ref, ga_ref, w_ref, gh_ref, cos_ref, sin_ref, o_ref, *, heads, dk, n_ctx_tiles):
    y = _rms(cq_ref[...], ga_ref[...])
    q = jnp.dot(y.astype(BF16), w_ref[...], preferred_element_type=F32)
    _store_heads_by_segment(q, gh_ref[...], o_ref, cos_ref, sin_ref, n_ctx_tiles, heads=heads, dk=dk)


def mla_q(dm, p2, ga, w_uq, gh, cos_t, sin_t):
    tm = dm.tm_s
    rb = _rope_blk(dm, tm)
    tab = pl.BlockSpec((tm, LANE), lambda i: (rb(i), 0))
    return pl.pallas_call(
        functools.partial(_q_kernel, heads=dm.heads, dk=dm.dk, n_ctx_tiles=dm.n_ctx // tm),
        out_shape=jax.ShapeDtypeStruct((dm.nt, dm.heads * LANE), BF16),
        grid=(dm.nt // tm,),
        in_specs=[pl.BlockSpec((tm, dm.q_lora), lambda i: (i, 0)),
                  pl.BlockSpec((1, dm.q_lora), lambda i: (0, 0)),
                  pl.BlockSpec((dm.q_lora, dm.heads * LANE), lambda i: (0, 0)),
                  pl.BlockSpec((1, LANE), lambda i: (0, 0)),
                  tab, tab],
        out_specs=pl.BlockSpec((tm, dm.heads * LANE), lambda i: (i, 0)),
        compiler_params=_cparams("arbitrary"),
        name="mla_q",
    )(p2, ga, w_uq, gh, cos_t, sin_t)


def _kv_kernel(ckv_ref, kpe_ref, ga_ref, w_ref, gk_ref, *rest, heads, dk, n_ctx_tiles, from_cache):
    k_ref, v_ref, cn_ref = rest[-3:]
    x = ckv_ref[...]
    if not from_cache:
        x = _rms(x, ga_ref[...])
    cn_ref[...] = x
    kv = jnp.dot(x.astype(BF16), w_ref[...], preferred_element_type=F32)
    kpe = kpe_ref[...]
    k = kv[:, :heads * LANE] + jnp.concatenate([kpe] * heads, axis=1)
    if from_cache:
        _store_heads(k, gk_ref[...], k_ref, None, heads=heads, dk=dk)
    else:
        _store_heads_by_segment(k, gk_ref[...], k_ref, rest[0], rest[1], n_ctx_tiles, heads=heads, dk=dk)
    v_ref[...] = kv[:, heads * LANE:].astype(BF16)


def mla_kv(dm, ckv_src, ckv_blk, kpe_src, kpe_blk, nrows, tm, ga, w_ukv, gk, rope=None):
    hk = dm.heads * LANE
    hv = dm.heads * dm.vh
    in_specs = [pl.BlockSpec((tm, dm.kv_lora), lambda i: (i, ckv_blk)),
                pl.BlockSpec((tm, LANE), lambda i: (i, kpe_blk)),
                pl.BlockSpec((1, dm.kv_lora), lambda i: (0, 0)),
                pl.BlockSpec((dm.kv_lora, hk + hv), lambda i: (0, 0)),
                pl.BlockSpec((1, LANE), lambda i: (0, 0))]
    args = [ckv_src, kpe_src, ga, w_ukv, gk]
    if rope is not None:
        rb = _rope_blk(dm, tm)
        in_specs += [pl.BlockSpec((tm, LANE), lambda i: (rb(i), 0))] * 2
        args += list(rope)
    return pl.pallas_call(
        functools.partial(_kv_kernel, heads=dm.heads, dk=dm.dk, n_ctx_tiles=dm.n_ctx // tm,
                          from_cache=rope is None),
        out_shape=(jax.ShapeDtypeStruct((nrows, hk), BF16),
                   jax.ShapeDtypeStruct((nrows, hv), BF16),
                   jax.ShapeDtypeStruct((nrows, dm.kv_lora), F32)),
        grid=(nrows // tm,),
        in_specs=in_specs,
        out_specs=(pl.BlockSpec((tm, hk), lambda i: (i, 0)),
                   pl.BlockSpec((tm, hv), lambda i: (i, 0)),
                   pl.BlockSpec((tm, dm.kv_lora), lambda i: (i, 0))),
        compiler_params=_cparams("arbitrary"),
        name="mla_kv_trunk" if rope is not None else "mla_kv_cache",
    )(*args)


def _attn_kernel(q_ref, *refs, nseg, pairs, vh):
    o_ref = refs[-1]
    nt_dims = (((1,), (1,)), ((), ()))
    lane = lax.broadcasted_iota(jnp.int32, (q_ref.shape[0], LANE), 1)
    for p in range(pairs):
        outs = []
        for hh in range(2):
            c0 = (2 * p + hh) * LANE
            q = q_ref[:, c0:c0 + LANE]
            m = l = acc = None
            for i in range(nseg):
                k_ref, v_ref = refs[2 * i], refs[2 * i + 1]
                kc = min(k_ref.shape[0], KEY_CHUNK)
                for c in range(k_ref.shape[0] // kc):
                    rows = slice(c * kc, (c + 1) * kc)
                    s = lax.dot_general(q, k_ref[rows, c0:c0 + LANE], nt_dims, preferred_element_type=F32)
                    v = v_ref[rows, p * LANE:(p + 1) * LANE]
                    ms = jnp.max(s, axis=-1, keepdims=True)
                    if m is None:
                        m = ms
                        e = jnp.exp(s - m)
                        l = jnp.sum(e, axis=-1, keepdims=True)
                        acc = jnp.dot(e.astype(BF16), v, preferred_element_type=F32)
                    else:
                        m_new = jnp.maximum(m, ms)
                        alpha = jnp.exp(m - m_new)
                        e = jnp.exp(s - m_new)
                        l = alpha * l + jnp.sum(e, axis=-1, keepdims=True)
                        acc = alpha * acc + jnp.dot(e.astype(BF16), v, preferred_element_type=F32)
                        m = m_new
            outs.append(acc / l)
        o_ref[:, p * LANE:(p + 1) * LANE] = jnp.where(lane < vh, outs[0], outs[1]).astype(BF16)


def attention(dm, q, segs, row0, nseq, length, tq, pairs, prev=None):
    npg = dm.heads // 2 // pairs
    nqt = length // tq
    qb0 = row0 // tq
    in_specs = [pl.BlockSpec((tq, pairs * 2 * LANE), lambda s, g, t: (qb0 + s * nqt + t, g))]
    args = [q]
    for (k, v, lk, r0) in segs:
        kb0 = r0 // lk
        in_specs.append(pl.BlockSpec((lk, pairs * 2 * LANE), lambda s, g, t, kb0=kb0: (kb0 + s, g)))
        in_specs.append(pl.BlockSpec((lk, pairs * LANE), lambda s, g, t, kb0=kb0: (kb0 + s, g)))
        args += [k, v]
    aliases = {}
    if prev is not None:
        in_specs.append(pl.BlockSpec(memory_space=pl.ANY))
        args.append(prev)
        aliases = {len(args) - 1: 0}
    return pl.pallas_call(
        functools.partial(_attn_kernel, nseg=len(segs), pairs=pairs, vh=dm.vh),
        out_shape=jax.ShapeDtypeStruct((dm.nt, dm.heads * dm.vh), BF16),
        grid=(nseq, npg, nqt),
        in_specs=in_specs,
        out_specs=pl.BlockSpec((tq, pairs * LANE), lambda s, g, t: (qb0 + s * nqt + t, g)),
        input_output_aliases=aliases,
        compiler_params=_cparams("arbitrary", "arbitrary", "arbitrary"),
        name="attention_%d" % length,
    )(*args)


def _conf_kernel(a_ref, g_ref, w_ref, b_ref, lg_ref, lb_ref, *rest, length, ktaps, sub):
    o_ref, hs_ref, sh_ref, w8_ref, pre_ref = rest[-5:]
    cw = a_ref.shape[1]
    off = HALO - (ktaps - 1) // 2
    span = sub + SUBLANE * ((off + ktaps - 1) // SUBLANE)
    groups = CONV_CHUNK // SUBLANE
    hs_ref[0:HALO, :] = jnp.zeros((HALO, cw), F32)
    hs_ref[HALO + length:, :] = jnp.zeros((HALO, cw), F32)
    hs_ref[HALO:HALO + length, :] = a_ref[...].astype(F32) * jax.nn.sigmoid(g_ref[...].astype(F32))
    for j in range(ktaps):
        w8_ref[j] = jnp.broadcast_to(w_ref[j:j + 1, :], (SUBLANE, cw))
    bias, lg, lb = b_ref[...], lg_ref[...], lb_ref[...]

    for sb in range(length // sub):
        base = sb * sub
        for r in range(SUBLANE):
            sh_ref[r] = hs_ref[base + r:base + r + span, :]

        def conv_body(c, carry):
            t0 = pl.multiple_of(c * CONV_CHUNK, CONV_CHUNK)
            acc = jnp.broadcast_to(bias, (groups, SUBLANE, cw))
            for j in range(ktaps):
                q, r = divmod(off + j, SUBLANE)
                xj = sh_ref[r, pl.ds(t0 + SUBLANE * q, CONV_CHUNK), :].reshape(groups, SUBLANE, cw)
                acc = acc + w8_ref[j] * xj
            pre_ref[pl.ds(t0, CONV_CHUNK), :] = acc.reshape(CONV_CHUNK, cw)
            return carry

        def norm_body(c, carry):
            t0 = pl.multiple_of(c * NORM_CHUNK, NORM_CHUNK)
            acc = pre_ref[pl.ds(t0, NORM_CHUNK), :]
            mu = jnp.mean(acc, axis=-1, keepdims=True)
            xc = acc - mu
            var = jnp.mean(xc * xc, axis=-1, keepdims=True)
            y = xc * lax.rsqrt(var + EPS) * lg + lb
            o_ref[pl.ds(base + t0, NORM_CHUNK), :] = _silu(y).astype(BF16)
            return carry

        lax.fori_loop(0, sub // CONV_CHUNK, conv_body, 0)
        lax.fori_loop(0, sub // NORM_CHUNK, norm_body, 0, unroll=4)


def conformer(dm, p, cf_col0, row0, nseq, length, w, b, lg, lb, prev=None):
    cw = dm.cw
    rb0 = row0 // length
    cb0 = cf_col0 // cw
    one = lambda s: (0, 0)
    in_specs = [pl.BlockSpec((length, cw), lambda s: (rb0 + s, cb0)),
                pl.BlockSpec((length, cw), lambda s: (rb0 + s, cb0 + 1)),
                pl.BlockSpec((dm.cf_k, cw), one), pl.BlockSpec((1, cw), one),
                pl.BlockSpec((1, cw), one), pl.BlockSpec((1, cw), one)]
    args = [p, p, w, b, lg, lb]
    aliases = {}
    if prev is not None:
        in_specs.append(pl.BlockSpec(memory_space=pl.ANY))
        args.append(prev)
        aliases = {len(args) - 1: 0}
    sub = min(length, CONV_SUB)
    span = sub + SUBLANE * ((HALO + (dm.cf_k - 1) // 2) // SUBLANE)
    return pl.pallas_call(
        functools.partial(_conf_kernel, length=length, ktaps=dm.cf_k, sub=sub),
        out_shape=jax.ShapeDtypeStruct((dm.nt, cw), BF16),
        grid=(nseq,),
        in_specs=in_specs,
        out_specs=pl.BlockSpec((length, cw), lambda s: (rb0 + s, 0)),
        scratch_shapes=[pltpu.VMEM((length + 2 * HALO, cw), F32),
                        pltpu.VMEM((SUBLANE, span, cw), F32),
                        pltpu.VMEM((dm.cf_k, SUBLANE, cw), F32),
                        pltpu.VMEM((sub, cw), F32)],
        input_output_aliases=aliases,
        compiler_params=_cparams("arbitrary"),
        name="conformer_%d" % length,
    )(*args)


def _merge_kernel(z_ref, o_ref, h_ref, ga_ref, gb_ref, gc_ref, wa_ref, wb_ref, wc_ref, bc_ref, m_ref):
    ya = jnp.dot(z_ref[...], wa_ref[...], preferred_element_type=F32)
    yb = jnp.dot(o_ref[...], wb_ref[...], preferred_element_type=F32)
    yc = jnp.dot(h_ref[...], wc_ref[...], preferred_element_type=F32) + bc_ref[...]
    sg = lambda r: jax.nn.sigmoid(r[...].astype(F32))
    m_ref[...] = (sg(ga_ref) * ya + sg(gb_ref) * yb + sg(gc_ref) * yc).astype(BF16)


def merge(dm, z2, o, hc, p, gate_col0, wa, wb, wc, bc):
    tm, d = dm.tm_s, dm.d
    gb0 = gate_col0 // d
    row = lambda i: (i, 0)
    one = lambda i: (0, 0)
    return pl.pallas_call(
        _merge_kernel,
        out_shape=jax.ShapeDtypeStruct((dm.nt, d), BF16),
        grid=(dm.nt // tm,),
        in_specs=[pl.BlockSpec((tm, dm.hw), row), pl.BlockSpec((tm, dm.heads * dm.vh), row),
                  pl.BlockSpec((tm, dm.cw), row),
                  pl.BlockSpec((tm, d), lambda i: (i, gb0)), pl.BlockSpec((tm, d), lambda i: (i, gb0 + 1)),
                  pl.BlockSpec((tm, d), lambda i: (i, gb0 + 2)),
                  _resident((dm.hw, d), one), _resident((dm.heads * dm.vh, d), one),
                  _resident((dm.cw, d), one), pl.BlockSpec((1, d), one)],
        out_specs=pl.BlockSpec((tm, d), row),
        compiler_params=_cparams("arbitrary"),
        name="merge",
    )(z2, o, hc, p, p, p, wa, wb, wc, bc)


def _outproj_kernel(m_ref, w_ref, *refs, d, n_x, n_ctx_tiles):
    mod_ref, g_ref, wr_ref, br_ref, x1_ref, h2_ref, lg_ref = refs[n_x:]
    y = jnp.dot(m_ref[...], w_ref[...], preferred_element_type=F32)
    g1 = mod_ref[0, :, 2 * d:3 * d]
    sh2 = mod_ref[0, :, 3 * d:4 * d]
    sc2 = mod_ref[0, :, 4 * d:5 * d]
    x1 = _token_rows(refs[:n_x], n_ctx_tiles) + g1 * y
    x1_ref[...] = x1
    h2 = _rms(x1, g_ref[...]) * (1.0 + sc2) + sh2
    h2_ref[...] = _pack_bf16_pairs(h2)
    lg_ref[...] = jnp.dot(h2.astype(BF16), wr_ref[...], preferred_element_type=F32) + br_ref[...]


def outproj(dm, merged, w_out, xs, mod_l, g, wr, br):
    tm, d = dm.tm_s, dm.d
    cid = _cond_id(dm, tm)
    row = lambda i: (i, 0)
    one = lambda i: (0, 0)
    return pl.pallas_call(
        functools.partial(_outproj_kernel, d=d, n_x=len(xs), n_ctx_tiles=dm.n_ctx // tm),
        out_shape=(jax.ShapeDtypeStruct((dm.nt, d), F32), jax.ShapeDtypeStruct((dm.nt, d // 2), U32),
                   jax.ShapeDtypeStruct((dm.nt, LANE), F32)),
        grid=(dm.nt // tm,),
        in_specs=[pl.BlockSpec((tm, d), row), _resident((d, d), one)] + _token_specs(dm, tm, xs) + [
                  pl.BlockSpec((1, 1, 6 * d), lambda i: (cid(i), 0, 0)), pl.BlockSpec((1, d), one),
                  pl.BlockSpec((d, LANE), one), pl.BlockSpec((1, LANE), one)],
        out_specs=(pl.BlockSpec((tm, d), row), pl.BlockSpec((tm, d // 2), row), pl.BlockSpec((tm, LANE), row)),
        compiler_params=_cparams("arbitrary"),
        name="outproj",
    )(merged, w_out, *xs, mod_l, g, wr, br)


def _route_kernel(lg_ref, info_ref, cnt_ref, carry_ref, *, n_exp, groups, epg):
    @pl.when(pl.program_id(0) == 0)
    def _():
        carry_ref[...] = jnp.zeros_like(carry_ref)

    x = lg_ref[...]
    lane = lax.broadcasted_iota(jnp.int32, x.shape, 1).astype(F32)
    big = jnp.float32(1e9)
    neg = jnp.float32(-jnp.inf)
    is_g = (lane >= n_exp) & (lane < n_exp + groups)
    xg = jnp.where(is_g, x, neg)
    mg = jnp.max(xg, axis=-1, keepdims=True)
    sg = jnp.sum(jnp.where(is_g, jnp.exp(xg - mg), 0.0), axis=-1, keepdims=True)
    pg_top = 1.0 / sg
    gidx = jnp.min(jnp.where(xg == mg, lane, big), axis=-1, keepdims=True) - n_exp
    lo = gidx * epg
    in_grp = (lane >= lo) & (lane < lo + epg)
    xe = jnp.where(in_grp, x, neg)
    m1 = jnp.max(xe, axis=-1, keepdims=True)
    e1 = jnp.min(jnp.where(xe == m1, lane, big), axis=-1, keepdims=True)
    xe2 = jnp.where(lane == e1, neg, xe)
    m2 = jnp.max(xe2, axis=-1, keepdims=True)
    e2 = jnp.min(jnp.where(xe2 == m2, lane, big), axis=-1, keepdims=True)
    t = jnp.exp(m2 - m1)
    w1 = pg_top / (1.0 + t)
    w2 = pg_top * t / (1.0 + t)
    tm = x.shape[0]
    o1 = jnp.where(lane == e1, 1.0, 0.0)
    o2 = jnp.where(lane == e2, 1.0, 0.0)
    rr = lax.broadcasted_iota(jnp.int32, (tm, tm), 0)
    cc = lax.broadcasted_iota(jnp.int32, (tm, tm), 1)
    tri = jnp.where(cc < rr, 1.0, 0.0).astype(BF16)
    cum1 = jnp.dot(tri, o1.astype(BF16), preferred_element_type=F32)
    cum2 = jnp.dot(tri, o2.astype(BF16), preferred_element_type=F32)
    tot1 = jnp.sum(o1, axis=0, keepdims=True)
    tot2 = jnp.sum(o2, axis=0, keepdims=True)
    carry = carry_ref[0:1, :]
    rank1 = jnp.sum(o1 * (carry + cum1), axis=-1, keepdims=True)
    rank2 = jnp.sum(o2 * (carry + tot1 + cum2), axis=-1, keepdims=True)
    new = jnp.broadcast_to(carry + tot1 + tot2, carry_ref.shape)
    carry_ref[...] = new
    cnt_ref[...] = new
    cols = (e1, e2, rank1, rank2, w1, w2)
    info = jnp.zeros_like(x)
    for k, col in enumerate(cols):
        info = jnp.where(lane == k, col, info)
    info_ref[...] = info


def route(dm, logits):
    tm = dm.tm_s
    return pl.pallas_call(
        functools.partial(_route_kernel, n_exp=dm.n_exp, groups=dm.groups, epg=dm.epg),
        out_shape=(jax.ShapeDtypeStruct((dm.nt, LANE), F32), jax.ShapeDtypeStruct((SUBLANE, LANE), F32)),
        grid=(dm.nt // tm,),
        in_specs=[pl.BlockSpec((tm, LANE), lambda i: (i, 0))],
        out_specs=(pl.BlockSpec((tm, LANE), lambda i: (i, 0)), pl.BlockSpec((SUBLANE, LANE), lambda i: (0, 0))),
        scratch_shapes=[pltpu.VMEM((SUBLANE, LANE), F32)],
        compiler_params=_cparams("arbitrary"),
        name="route",
    )(logits)


def _positions_kernel(info_ref, cnt_ref, pos_ref, meta_ref, *, n_exp, tile):
    cnt = cnt_ref[0:1, :]
    padded = jnp.floor((cnt + (tile - 1.0)) * (1.0 / tile)) * tile
    a = lax.broadcasted_iota(jnp.int32, (LANE, LANE), 0)
    b = lax.broadcasted_iota(jnp.int32, (LANE, LANE), 1)
    before = jnp.where(a < b, 1.0, 0.0).astype(BF16)
    off = jnp.dot(jnp.broadcast_to(padded, (SUBLANE, LANE)).astype(BF16), before,
                  preferred_element_type=F32)[0:1, :]
    info = info_ref[...]
    lane = lax.broadcasted_iota(jnp.int32, info.shape, 1).astype(F32)
    e1, e2, r1, r2 = info[:, 0:1], info[:, 1:2], info[:, 2:3], info[:, 3:4]
    p1 = r1 + jnp.sum(jnp.where(lane == e1, off, 0.0), axis=-1, keepdims=True)
    p2 = r2 + jnp.sum(jnp.where(lane == e2, off, 0.0), axis=-1, keepdims=True)
    pos_ref[...] = jnp.where(lane == 0, p1, jnp.where(lane == 1, p2, 0.0)).astype(jnp.int32)

    end = off + padded
    end_col = jnp.sum(jnp.where(a == b, jnp.broadcast_to(end, (LANE, LANE)), 0.0), axis=1, keepdims=True)
    start = lax.broadcasted_iota(jnp.int32, (1, 2 * LANE), 1).astype(F32) * tile
    te = jnp.sum(jnp.where(end_col <= start, 1.0, 0.0), axis=0, keepdims=True)
    te = jnp.minimum(te, n_exp - 1.0)
    n_used = jnp.sum(padded, axis=-1, keepdims=True) * (1.0 / tile)
    tail = jnp.maximum(end - tile, 0.0)
    meta_ref[...] = jnp.zeros_like(meta_ref)
    meta_ref[0:1, :] = te.astype(jnp.int32)
    meta_ref[1:2, 0:LANE] = tail.astype(jnp.int32)
    meta_ref[2:3, :] = jnp.broadcast_to(n_used, (1, 2 * LANE)).astype(jnp.int32)


def positions(dm, info, cnt):
    tm = dm.tm_s
    return pl.pallas_call(
        functools.partial(_positions_kernel, n_exp=dm.n_exp, tile=dm.moe_tile),
        out_shape=(jax.ShapeDtypeStruct((dm.nt, LANE), jnp.int32),
                   jax.ShapeDtypeStruct((SUBLANE, 2 * LANE), jnp.int32)),
        grid=(dm.nt // tm,),
        in_specs=[pl.BlockSpec((tm, LANE), lambda i: (i, 0)), pl.BlockSpec((SUBLANE, LANE), lambda i: (0, 0))],
        out_specs=(pl.BlockSpec((tm, LANE), lambda i: (i, 0)),
                   pl.BlockSpec((SUBLANE, 2 * LANE), lambda i: (0, 0))),
        compiler_params=_cparams("arbitrary"),
        name="positions",
    )(info, cnt)


def _row_copy(src, src_row, dst, dst_row, sem):
    return pltpu.make_async_copy(src.at[pl.ds(src_row, 1)], dst.at[pl.ds(dst_row, 1)], sem)


def _dispatch_kernel(tail_ref, pos1_ref, pos2_ref, h_ref, xs_ref, zero_ref, sem, zsem, *, n_exp, tile, tb):
    i = pl.program_id(0)

    def zero_copy(e):
        start = pl.multiple_of(tail_ref[e], SUBLANE)
        return pltpu.make_async_copy(zero_ref, xs_ref.at[pl.ds(start, tile)], zsem)

    @pl.when(i == 0)
    def _():
        zero_ref[...] = jnp.zeros_like(zero_ref)
        for e in range(n_exp):
            zero_copy(e).start()
        for e in range(n_exp):
            zero_copy(e).wait()

    def body(t, carry):
        _row_copy(h_ref, t, xs_ref, pos1_ref[0, 0, t], sem).start()
        _row_copy(h_ref, t, xs_ref, pos2_ref[0, 0, t], sem).start(priority=1)
        return carry

    lax.fori_loop(0, tb, body, 0, unroll=8)
    pltpu.make_async_copy(xs_ref.at[pl.ds(0, 2 * tb)], xs_ref.at[pl.ds(0, 2 * tb)], sem).wait()


def dispatch(dm, tail, pos1, pos2, h2p):
    tb, tile = dm.tm_s, dm.moe_tile
    rows = 2 * dm.nt + dm.n_exp * tile
    blk = pl.BlockSpec((1, 1, tb), lambda i, tail: (i, 0, 0), memory_space=pltpu.SMEM)
    return pl.pallas_call(
        functools.partial(_dispatch_kernel, n_exp=dm.n_exp, tile=tile, tb=tb),
        out_shape=jax.ShapeDtypeStruct((rows, dm.d // 2), U32),
        grid_spec=pltpu.PrefetchScalarGridSpec(
            num_scalar_prefetch=1, grid=(dm.nt // tb,),
            in_specs=[blk, blk, pl.BlockSpec((tb, dm.d // 2), lambda i, tail: (i, 0))],
            out_specs=pl.BlockSpec(memory_space=pl.ANY),
            scratch_shapes=[pltpu.VMEM((tile, dm.d // 2), U32), pltpu.SemaphoreType.DMA,
                            pltpu.SemaphoreType.DMA]),
        compiler_params=_cparams("arbitrary"),
        name="dispatch",
    )(tail, pos1, pos2, h2p)


def _experts_kernel(te_ref, nu_ref, x_ref, wg_ref, wu_ref, wd_ref, y_ref, wgb_ref, wub_ref, wdb_ref):
    r = pl.program_id(0)

    @pl.when(r < nu_ref[0])
    def _():
        @pl.when((r == 0) | (te_ref[r] != te_ref[jnp.maximum(r - 1, 0)]))
        def _():
            wgb_ref[...] = wg_ref[0, 0].astype(BF16)
            wub_ref[...] = wu_ref[0, 0].astype(BF16)
            wdb_ref[...] = wd_ref[0, 0].astype(BF16)

        x = _unpack_bf16_pairs(x_ref[...]).astype(BF16)
        hg = jnp.dot(x, wgb_ref[...], preferred_element_type=F32)
        hu = jnp.dot(x, wub_ref[...], preferred_element_type=F32)
        a = (_silu(hg) * hu).astype(BF16)
        y_ref[...] = _pack_bf16_pairs(jnp.dot(a, wdb_ref[...], preferred_element_type=F32))


def experts(dm, layer, te, n_used, xs, w_eg, w_eu, w_ed):
    tile, d, fe = dm.moe_tile, dm.d, dm.fe
    rows = xs.shape[0]
    last = lambda r, nu: jnp.minimum(r, nu[0] - 1)
    wmap = lambda r, te, nu: (layer, te[last(r, nu)], 0, 0)
    return pl.pallas_call(
        _experts_kernel,
        out_shape=jax.ShapeDtypeStruct((rows, d // 2), U32),
        grid_spec=pltpu.PrefetchScalarGridSpec(
            num_scalar_prefetch=2, grid=(rows // tile,),
            in_specs=[pl.BlockSpec((tile, d // 2), lambda r, te, nu: (last(r, nu), 0)),
                      pl.BlockSpec((1, 1, d, fe), wmap), pl.BlockSpec((1, 1, d, fe), wmap),
                      pl.BlockSpec((1, 1, fe, d), wmap)],
            out_specs=pl.BlockSpec((tile, d // 2), lambda r, te, nu: (last(r, nu), 0)),
            scratch_shapes=[pltpu.VMEM((d, fe), BF16), pltpu.VMEM((d, fe), BF16), pltpu.VMEM((fe, d), BF16)]),
        compiler_params=_cparams("arbitrary"),
        name="experts",
    )(te, n_used, xs, w_eg, w_eu, w_ed)


def _combine_kernel(pos1_ref, pos2_ref, ys_ref, info_ref, x1_ref, mod_ref, *rest, d, tb, n_ctx_tiles):
    b1_ref, b2_ref, sem = rest[-3:]
    o_refs = rest[:-3]

    def body(t, carry):
        _row_copy(ys_ref, pos1_ref[0, 0, t], b1_ref, t, sem).start()
        _row_copy(ys_ref, pos2_ref[0, 0, t], b2_ref, t, sem).start(priority=1)
        return carry

    lax.fori_loop(0, tb, body, 0, unroll=8)
    pltpu.make_async_copy(ys_ref.at[pl.ds(0, tb)], b1_ref, sem).wait()
    pltpu.make_async_copy(ys_ref.at[pl.ds(0, tb)], b2_ref, sem).wait()
    info = info_ref[...]
    y = info[:, 4:5] * _unpack_bf16_pairs(b1_ref[...]) + info[:, 5:6] * _unpack_bf16_pairs(b2_ref[...])
    out = x1_ref[...] + mod_ref[0, :, 5 * d:6 * d] * y
    if len(o_refs) == 1:
        o_refs[0][...] = out
    else:
        is_ctx = pl.program_id(0) < n_ctx_tiles

        @pl.when(is_ctx)
        def _():
            o_refs[0][...] = out

        @pl.when(jnp.logical_not(is_ctx))
        def _():
            o_refs[1][...] = out


def combine(dm, pos1, pos2, ys, info, x1, mod_l, split=False):
    tb, d = dm.tm_s, dm.d
    cid = _cond_id(dm, tb)
    blk = pl.BlockSpec((1, 1, tb), lambda i: (i, 0, 0), memory_space=pltpu.SMEM)
    row = lambda i: (i, 0)
    if split:
        out_shape = (jax.ShapeDtypeStruct((dm.n_ctx, d), F32), jax.ShapeDtypeStruct((dm.n_lat, d), F32))
        out_specs = tuple(_token_specs(dm, tb, out_shape))
    else:
        out_shape = jax.ShapeDtypeStruct((dm.nt, d), F32)
        out_specs = pl.BlockSpec((tb, d), row)
    return pl.pallas_call(
        functools.partial(_combine_kernel, d=d, tb=tb, n_ctx_tiles=dm.n_ctx // tb),
        out_shape=out_shape,
        grid=(dm.nt // tb,),
        in_specs=[blk, blk, pl.BlockSpec(memory_space=pl.ANY), pl.BlockSpec((tb, LANE), row),
                  pl.BlockSpec((tb, d), row), pl.BlockSpec((1, 1, 6 * d), lambda i: (cid(i), 0, 0))],
        out_specs=out_specs,
        scratch_shapes=[pltpu.VMEM((tb, d // 2), U32), pltpu.VMEM((tb, d // 2), U32),
                        pltpu.SemaphoreType.DMA],
        compiler_params=_cparams("arbitrary"),
        name="combine",
    )(pos1, pos2, ys, info, x1, mod_l)


def _dft_kernel(c1_ref, s1_ref, c2_ref, s2_ref, c_ref, s_ref):
    c1, s1, c2, s2 = c1_ref[0], s1_ref[0], c2_ref[...], s2_ref[...]
    c_ref[...] = (c1 * c2 - s1 * s2).astype(BF16)
    s_ref[...] = (s1 * c2 + c1 * s2).astype(BF16)


def _dft_tables(length):
    blk = min(length, 256)
    nblk = length // blk
    s = jnp.arange(length, dtype=jnp.int32)

    def angle(k):
        return ((k[:, None] * s[None, :]) % (2 * length)).astype(F32) * (math.pi / length)

    a1 = angle(jnp.arange(nblk, dtype=jnp.int32) * blk)
    a2 = angle(jnp.arange(blk, dtype=jnp.int32))
    row = pl.BlockSpec((1, 1, length), lambda a: (a, 0, 0))
    full = pl.BlockSpec((blk, length), lambda a: (0, 0))
    out = jax.ShapeDtypeStruct((length, length), BF16)
    return pl.pallas_call(
        _dft_kernel,
        out_shape=(out, out),
        grid=(nblk,),
        in_specs=[row, row, full, full],
        out_specs=(pl.BlockSpec((blk, length), lambda a: (a, 0)), pl.BlockSpec((blk, length), lambda a: (a, 0))),
        compiler_params=_cparams("arbitrary"),
        name="dft_tables_%d" % length,
    )(jnp.cos(a1)[:, None, :], jnp.sin(a1)[:, None, :], jnp.cos(a2), jnp.sin(a2))


def _head_layout(dm, nope, rope):
    half = dm.rope // 2
    a = LANE // 2 - half
    ref = nope if nope is not None else rope
    z = lambda n: jnp.zeros(ref.shape[:-1] + (n,), ref.dtype)
    r1, r2 = (rope[..., :half], rope[..., half:]) if rope is not None else (z(half), z(half))
    n1, n2 = (nope[..., :a], nope[..., a:]) if nope is not None else (z(a), z(dm.nope - a))
    return jnp.concatenate([r1, n1, r2, n2, z(LANE - dm.dk)], axis=-1)


def _rope_tables(dm):
    length = dm.dec_seq
    n_freq = dm.rope // 4
    pos = jnp.arange(length, dtype=jnp.int32)
    row = (pos // dm.grid_w).astype(F32)
    col = (pos % dm.grid_w).astype(F32)
    inv = jnp.power(ROPE_BASE, -jnp.arange(n_freq, dtype=F32) / n_freq)
    ang = jnp.concatenate([row[:, None] * inv[None, :], col[:, None] * inv[None, :]], axis=-1)
    cos, sin = jnp.cos(ang), jnp.sin(ang)
    cos_t = _head_layout(dm, jnp.ones((length, dm.nope), F32), jnp.concatenate([cos, cos], axis=1))
    sin_t = _head_layout(dm, None, jnp.concatenate([-sin, sin], axis=1))
    return cos_t, sin_t


def _pack_w_in(dm, w_in):
    o = 0
    parts = {}
    for name, n in (("hy", 3 * dm.hw), ("cq", dm.q_lora), ("ckv", dm.kv_lora), ("kpe", dm.rope),
                    ("cf", 2 * dm.cw), ("gate", 3 * dm.d)):
        parts[name] = w_in[:, o:o + n]
        o += n
    used = dm.q_lora + dm.kv_lora + LANE
    small = [parts["cq"], parts["ckv"], _head_layout(dm, None, parts["kpe"]),
             jnp.zeros((w_in.shape[0], dm.tn - used), w_in.dtype)]
    return jnp.concatenate(small + [parts["gate"], parts["hy"], parts["cf"]], axis=1).astype(BF16)


def _forward(dm, x_prompt, x_sample, cache_ckv, cache_kpe, c, c_ctx, w_mod, b_mod, norm_mix, norm_ffn,
             w_in, hy_short, hy_f_w1, hy_f_b1, hy_f_w2, hy_f_b2, hy_f_w3, hy_decay, hy_bias, w_hy_out,
             q_a_norm, w_uq, kv_a_norm, w_ukv, q_norm, k_norm, w_mla_out,
             cf_dw, cf_dw_b, cf_ln_g, cf_ln_b, w_cf_out, b_cf_out, w_out,
             w_rg, b_rg, w_re, b_re, w_eg, w_eu, w_ed):
    d, hw, cw, heads = dm.d, dm.hw, dm.cw, dm.heads
    n_ctx, nt = dm.n_ctx, dm.nt
    gate_col0, hy_col0, cf_col0 = 0, 3 * d, 3 * d + 3 * hw
    ckv_blk = dm.q_lora // dm.kv_lora
    kpe_blk = (dm.q_lora + dm.kv_lora) // LANE
    kpe_lane0 = dm.q_lora + dm.kv_lora
    half = dm.rope // 2

    xs = (jnp.concatenate([x_prompt.reshape(n_ctx, d), x_sample.reshape(dm.n_lat, d)], axis=0),)
    cond8 = jnp.concatenate([c_ctx[None, :], c, jnp.zeros((8 - 1 - dm.dec_batch, d), F32)], axis=0)
    mod = adaln(dm, cond8, w_mod, b_mod)

    dft = {dm.seq: _dft_tables(dm.seq), dm.dec_seq: _dft_tables(dm.dec_seq)}
    cos_t, sin_t = _rope_tables(dm)
    scale = float(dm.dk) ** -0.5
    tc_ctx = min(hw, 1024)
    tq_ctx = dm.seq
    pairs_ctx = heads // 2
    tm_cache = min(dm.dec_batch * dm.past, dm.tm_s)

    ckv_out, kpe_out = [], []
    for l in range(dm.depth):
        mod_l = mod[l][:, None, :]
        w_packed = _pack_w_in(dm, w_in[l])
        p2, p = inproj(dm, xs, mod_l, norm_mix[l][None, :], w_packed)

        w1p = jnp.zeros((LANE, dm.hy_fhid), F32).at[:1 + 2 * dm.hy_freqs].set(hy_f_w1[l])
        z2 = None
        for (length, row0, nseq, tc, nb) in ((dm.seq, 0, dm.batch, tc_ctx, 1),
                                             (dm.dec_seq, n_ctx, dm.dec_batch, min(hw, dm.tc_hy), dm.hy_nb)):
            cmat, smat = dft[length]
            filt = hyena_filters(dm, length, w1p, hy_f_b1[l][None, :], hy_f_w2[l], hy_f_b2[l][None, :],
                                 hy_f_w3[l], hy_decay[l][None, :])
            kc, ks, kn = hyena_spectra(dm, length, filt, cmat, smat)
            z2 = hyena_conv(dm, p, hy_col0, row0, nseq, length, tc, nb, hy_short[l], hy_bias[l], kc, ks, kn,
                            cmat, smat, prev=z2)

        wq = w_uq[l].reshape(dm.q_lora, heads, dm.dk)
        w_uq_p = _head_layout(dm, wq[..., :dm.nope], wq[..., dm.nope:])
        w_uq_p = w_uq_p.reshape(dm.q_lora, heads * LANE).astype(BF16)
        wkv = w_ukv[l].reshape(dm.kv_lora, heads, dm.nope + dm.vh)
        wk_p = _head_layout(dm, wkv[:, :, :dm.nope], None)
        w_ukv_p = jnp.concatenate([wk_p.reshape(dm.kv_lora, heads * LANE),
                                   wkv[:, :, dm.nope:].reshape(dm.kv_lora, heads * dm.vh)], axis=1).astype(BF16)
        gq = _head_layout(dm, q_norm[l][:dm.nope] * scale, q_norm[l][dm.nope:] * scale)[None, :]
        gk = _head_layout(dm, k_norm[l][:dm.nope], k_norm[l][dm.nope:])[None, :]
        q = mla_q(dm, p2, q_a_norm[l][None, :], w_uq_p, gq, cos_t, sin_t)
        k, v, ckv_n = mla_kv(dm, p2, ckv_blk, p2, kpe_blk, nt, dm.tm_s, kv_a_norm[l][None, :], w_ukv_p, gk,
                             rope=(cos_t, sin_t))
        kpe_c = _head_layout(dm, None, cache_kpe[:, l].reshape(dm.dec_batch * dm.past, dm.rope))
        k_c, v_c, _ = mla_kv(dm, cache_ckv[:, l].reshape(dm.dec_batch * dm.past, dm.kv_lora), 0, kpe_c, 0,
                             dm.dec_batch * dm.past, tm_cache, kv_a_norm[l][None, :], w_ukv_p, gk)
        o = attention(dm, q, [(k, v, dm.seq, 0)], 0, dm.batch, dm.seq, tq_ctx, pairs_ctx)
        o = attention(dm, q, [(k, v, dm.dec_seq, n_ctx), (k_c, v_c, dm.past, 0)], n_ctx, dm.dec_batch,
                      dm.dec_seq, min(dm.tq, dm.dec_seq), 1, prev=o)
        ckv_out.append(ckv_n[:n_ctx].reshape(dm.batch, dm.seq, dm.kv_lora))
        kpe_blk_ctx = p2[:n_ctx, kpe_lane0:kpe_lane0 + LANE]
        kpe_out.append(jnp.concatenate([kpe_blk_ctx[:, :half], kpe_blk_ctx[:, LANE // 2:LANE // 2 + half]],
                                       axis=1).reshape(dm.batch, dm.seq, dm.rope))

        hc = conformer(dm, p, cf_col0, 0, dm.batch, dm.seq, cf_dw[l], cf_dw_b[l][None, :],
                       cf_ln_g[l][None, :], cf_ln_b[l][None, :])
        hc = conformer(dm, p, cf_col0, n_ctx, dm.dec_batch, dm.dec_seq, cf_dw[l], cf_dw_b[l][None, :],
                       cf_ln_g[l][None, :], cf_ln_b[l][None, :], prev=hc)

        merged = merge(dm, z2, o, hc, p, gate_col0, w_hy_out[l].astype(BF16), w_mla_out[l].astype(BF16),
                       w_cf_out[l].astype(BF16), b_cf_out[l][None, :])
        wr = jnp.concatenate([w_re[l], w_rg[l], jnp.zeros((d, LANE - dm.n_exp - dm.groups), F32)],
                             axis=1).astype(BF16)
        br = jnp.concatenate([b_re[l], b_rg[l], jnp.zeros((LANE - dm.n_exp - dm.groups,), F32)])[None, :]
        x1, h2p, logits = outproj(dm, merged, w_out[l].astype(BF16), xs, mod_l, norm_ffn[l][None, :], wr, br)

        info, cnt = route(dm, logits)
        pos, meta = positions(dm, info, cnt)
        pos1 = pos[:, 0].reshape(nt // dm.tm_s, 1, dm.tm_s)
        pos2 = pos[:, 1].reshape(nt // dm.tm_s, 1, dm.tm_s)
        n_tiles = (2 * nt + dm.n_exp * dm.moe_tile) // dm.moe_tile
        xs = dispatch(dm, meta[1, :dm.n_exp], pos1, pos2, h2p)
        ys = experts(dm, l, meta[0, :n_tiles], meta[2, :1], xs, w_eg, w_eu, w_ed)
        last = l == dm.depth - 1
        xs = combine(dm, pos1, pos2, ys, info, x1, mod_l, split=last)
        xs = xs if last else (xs,)

    y_prompt = xs[0].reshape(dm.batch, dm.seq, d)
    y_sample = xs[1].reshape(dm.dec_batch, dm.dec_seq, d)
    return y_prompt, y_sample, jnp.stack(ckv_out, axis=1), jnp.stack(kpe_out, axis=1)


def kernel(x_prompt, x_sample, cache_ckv, cache_kpe, c, c_ctx, w_mod, b_mod, norm_mix, norm_ffn, w_in, hy_short, hy_f_w1, hy_f_b1, hy_f_w2, hy_f_b2, hy_f_w3, hy_decay, hy_bias, w_hy_out, q_a_norm, w_uq, kv_a_norm, w_ukv, q_norm, k_norm, w_mla_out, cf_dw, cf_dw_b, cf_ln_g, cf_ln_b, w_cf_out, b_cf_out, w_out, w_rg, b_rg, w_re, b_re, w_eg, w_eu, w_ed):
    return _forward(Dims(), x_prompt, x_sample, cache_ckv, cache_kpe, c, c_ctx, w_mod, b_mod, norm_mix,
                    norm_ffn, w_in, hy_short, hy_f_w1, hy_f_b1, hy_f_w2, hy_f_b2, hy_f_w3, hy_decay, hy_bias,
                    w_hy_out, q_a_norm, w_uq, kv_a_norm, w_ukv, q_norm, k_norm, w_mla_out,
                    cf_dw, cf_dw_b, cf_ln_g, cf_ln_b, w_cf_out, b_cf_out, w_out,
                    w_rg, b_rg, w_re, b_re, w_eg, w_eu, w_ed)
```
